```python
import math
import jax, jax.numpy as jnp
from jax import lax
import numpy as np

D_MODEL = 1024
BATCH = 4
SEQ = 4096
DEPTH = 4

PLE_DIM = 256
ATTN_HEADS = 8
ATTN_HEAD_DIM = 64
ATTN_WIDTH = ATTN_HEADS * ATTN_HEAD_DIM
Q_BLOCK = 128
POOL_WINDOWS = (2, 4, 8, 16)
POOL_GROUPS = len(POOL_WINDOWS)
POOL_GROUP_DIM = 128
POOL_WIDTH = POOL_GROUPS * POOL_GROUP_DIM
DN_HEADS = 4
DN_HEAD_DIM = 128
DN_WIDTH = DN_HEADS * DN_HEAD_DIM
DN_CONV = 4
DN_CHUNK = 64
N_GROUPS = 4
EXPERTS_PER_GROUP = 8
N_EXPERTS = N_GROUPS * EXPERTS_PER_GROUP
TOP_K = 2
D_EXPERT = 512
MOE_BLOCK = 128
DEEPNORM_ALPHA = (2 * DEPTH) ** 0.25
DEEPNORM_BETA = (8 * DEPTH) ** -0.25
LN_EPS = 1e-5
NORM_EPS = 1e-6
NEG_INF = -1e30
IN_SECTIONS = (ATTN_WIDTH, ATTN_WIDTH, ATTN_WIDTH, ATTN_HEADS,
               POOL_WIDTH,
               DN_WIDTH, DN_WIDTH, DN_WIDTH, DN_HEADS, DN_HEADS,
               DN_WIDTH,
               D_MODEL, D_MODEL, D_MODEL)
IN_WIDTH = sum(IN_SECTIONS)
IN_SPLITS = tuple(int(s) for s in np.cumsum(IN_SECTIONS)[:-1])

kernel_name = "hybrid_fox_pool_deltanet_hmoe_deepnorm"


def layer_norm(x, g, b):
    x32 = x.astype(jnp.float32)
    mu = jnp.mean(x32, axis=-1, keepdims=True)
    var = jnp.mean(jnp.square(x32 - mu), axis=-1, keepdims=True)
    y = (x32 - mu) * lax.rsqrt(var + LN_EPS) * g.astype(jnp.float32) + b.astype(jnp.float32)
    return y.astype(x.dtype)


def l2_normalize(t):
    return t * lax.rsqrt(jnp.sum(jnp.square(t), axis=-1, keepdims=True) + NORM_EPS)


def forgetting_attention(q, k, v, log_f):
    B, S, H, Dh = q.shape
    nb = S // Q_BLOCK
    c = jnp.cumsum(log_f.astype(jnp.float32), axis=1)
    c_k = jnp.transpose(c, (0, 2, 1))
    q_blocks = jnp.moveaxis(q.reshape(B, nb, Q_BLOCK, H, Dh), 1, 0)
    c_blocks = jnp.moveaxis(c_k.reshape(B, H, nb, Q_BLOCK), 2, 0)
    key_pos = jnp.arange(S)
    scale = Dh ** -0.5

    def one_block(args):
        q_i, c_i, blk = args
        query_pos = blk * Q_BLOCK + jnp.arange(Q_BLOCK)
        s = jnp.einsum("bqhd,bkhd->bhqk", q_i, k).astype(jnp.float32) * scale
        s = s + c_i[..., :, None] - c_k[:, :, None, :]
        s = jnp.where(key_pos[None, :] <= query_pos[:, None], s, NEG_INF)
        probs = jax.nn.softmax(s, axis=-1).astype(v.dtype)
        return jnp.einsum("bhqk,bkhd->bqhd", probs, v)

    out = lax.map(one_block, (q_blocks, c_blocks, jnp.arange(nb)))
    return jnp.moveaxis(out, 0, 1).reshape(B, S, H * Dh)


def multiscale_pool(u, pool_w, pool_scale):
    B, S, _ = u.shape
    u32 = u.astype(jnp.float32)
    cs = jnp.cumsum(u32, axis=1)
    n_pos = jnp.arange(1, S + 1, dtype=jnp.float32)[None, :, None]
    groups = []
    for g, w in enumerate(POOL_WINDOWS):
        sl = slice(g * POOL_GROUP_DIM, (g + 1) * POOL_GROUP_DIM)
        cs_g = cs[..., sl]
        cs_prev = jnp.pad(cs_g, ((0, 0), (w, 0), (0, 0)))[:, :S]
        mean = (cs_g - cs_prev) / jnp.minimum(n_pos, float(w))
        groups.append(mean - u32[..., sl])
    d = jnp.stack(groups, axis=2).astype(u.dtype)
    y = jnp.einsum("bsgc,gcd->bsgd", d, pool_w).reshape(B, S, POOL_WIDTH)
    return y * pool_scale


def causal_depthwise_conv(u, w):
    return lax.conv_general_dilated(
        u, w[:, None, :].astype(u.dtype), window_strides=(1,),
        padding=((DN_CONV - 1, 0),), dimension_numbers=("NWC", "WIO", "NWC"),
        feature_group_count=u.shape[-1])


def gated_delta_rule(q, k, v, g, beta):
    B, S, H, Dk = q.shape
    Dv = v.shape[-1]
    nc = S // DN_CHUNK

    def to_chunks(t):
        return jnp.transpose(t.reshape(B, nc, DN_CHUNK, H, t.shape[-1]), (0, 3, 1, 2, 4))

    q = to_chunks(q) * Dk ** -0.5
    k = to_chunks(k)
    v = to_chunks(v)
    beta = to_chunks(beta[..., None])
    gc = jnp.cumsum(to_chunks(g[..., None])[..., 0], axis=-1)
    idx = jnp.arange(DN_CHUNK)
    causal = idx[:, None] >= idx[None, :]
    strict = idx[:, None] > idx[None, :]
    decay = jnp.where(causal, jnp.exp(jnp.where(causal, gc[..., :, None] - gc[..., None, :], 0.0)), 0.0)
    kk = jnp.einsum("bhnid,bhnjd->bhnij", k, k)
    lower = jnp.where(strict, beta * kk * decay, 0.0)
    rhs = jnp.concatenate([v * beta, k * beta * jnp.exp(gc)[..., None]], axis=-1)
    sol = lax.linalg.triangular_solve(lower, rhs, left_side=True, lower=True, unit_diagonal=True)
    u_intra, w_state = sol[..., :Dv], sol[..., Dv:]
    attn = jnp.where(causal, jnp.einsum("bhnid,bhnjd->bhnij", q, k) * decay, 0.0)
    q_dec = q * jnp.exp(gc)[..., None]
    g_last = gc[..., -1]
    k_dec = k * jnp.exp(g_last[..., None] - gc)[..., None]
    xs = tuple(jnp.moveaxis(t, 2, 0) for t in (u_intra, w_state, attn, q_dec, k_dec, g_last))

    def step(state, inp):
        u_c, w_c, a_c, qd_c, kd_c, gl_c = inp
        u = u_c - jnp.einsum("bhck,bhkv->bhcv", w_c, state)
        o = jnp.einsum("bhck,bhkv->bhcv", qd_c, state) + jnp.einsum("bhij,bhjv->bhiv", a_c, u)
        state = state * jnp.exp(gl_c)[..., None, None] + jnp.einsum("bhck,bhcv->bhkv", kd_c, u)
        return state, o

    state0 = jnp.zeros((B, H, Dk, Dv), jnp.float32)
    _, o = lax.scan(step, state0, xs)
    return jnp.transpose(o, (1, 0, 3, 2, 4)).reshape(B, S, H, Dv)


def gated_deltanet(dq, dk, dv, da, db, dg, conv_w, a_log, dt_bias, norm_w):
    B, S, _ = dq.shape
    hd = (B, S, DN_HEADS, DN_HEAD_DIM)
    qkv = jax.nn.silu(causal_depthwise_conv(jnp.concatenate([dq, dk, dv], axis=-1), conv_w))
    q, k, v = jnp.split(qkv.astype(jnp.float32), 3, axis=-1)
    q = l2_normalize(q.reshape(hd))
    k = l2_normalize(k.reshape(hd))
    v = v.reshape(hd)
    beta = jax.nn.sigmoid(db.astype(jnp.float32))
    g = -jnp.exp(a_log.astype(jnp.float32)) * jax.nn.softplus(
        da.astype(jnp.float32) + dt_bias.astype(jnp.float32))
    o = gated_delta_rule(q, k, v, g, beta)
    o = o * lax.rsqrt(jnp.mean(jnp.square(o), axis=-1, keepdims=True) + NORM_EPS) * norm_w.astype(jnp.float32)
    o = o * jax.nn.silu(dg.astype(jnp.float32).reshape(hd))
    return o.reshape(B, S, DN_WIDTH).astype(dq.dtype)


def token_mixer(h, w_in, b_forget, pool_w, pool_scale, dn_conv, dn_a_log, dn_dt_bias,
                dn_norm_w, w_br_attn, w_br_pool, w_br_dn, w_out):
    B, S, _ = h.shape
    proj = h @ w_in
    (aq, ak, av, af, pu, dq, dk, dv, da, db, dg, ga, gp, gd) = jnp.split(proj, IN_SPLITS, axis=-1)
    ahd = (B, S, ATTN_HEADS, ATTN_HEAD_DIM)
    log_f = jax.nn.log_sigmoid((af + b_forget).astype(jnp.float32))
    y_attn = forgetting_attention(aq.reshape(ahd), ak.reshape(ahd), av.reshape(ahd), log_f)
    y_pool = multiscale_pool(pu, pool_w, pool_scale)
    y_dn = gated_deltanet(dq, dk, dv, da, db, dg, dn_conv, dn_a_log, dn_dt_bias, dn_norm_w)
    merged = (jax.nn.sigmoid(ga) * (y_attn @ w_br_attn)
              + jax.nn.sigmoid(gp) * (y_pool @ w_br_pool)
              + jax.nn.sigmoid(gd) * (y_dn @ w_br_dn))
    return merged @ w_out


def hierarchical_moe(h, w_rg, b_rg, w_re, b_re, w_gate, w_up, w_down):
    B, S, D = h.shape
    T = B * S
    hf = h.reshape(T, D)
    group_logits = (hf @ w_rg + b_rg).astype(jnp.float32)
    group = jnp.argmax(group_logits, axis=-1)
    p_group = jnp.take_along_axis(jax.nn.softmax(group_logits, axis=-1), group[:, None], axis=-1)
    expert_logits = (hf @ w_re + b_re).astype(jnp.float32).reshape(T, N_GROUPS, EXPERTS_PER_GROUP)
    local_logits = jnp.take_along_axis(expert_logits, group[:, None, None], axis=1)[:, 0]
    top_p, top_local = lax.top_k(jax.nn.softmax(local_logits, axis=-1), TOP_K)
    gate = p_group * top_p / jnp.sum(top_p, axis=-1, keepdims=True)
    expert = group[:, None] * EXPERTS_PER_GROUP + top_local

    n_assign = T * TOP_K
    flat_e = expert.reshape(-1)
    flat_tok = jnp.repeat(jnp.arange(T, dtype=jnp.int32), TOP_K)
    flat_gate = gate.reshape(-1)
    order = jnp.argsort(flat_e)
    sorted_e = flat_e[order]
    counts = jnp.bincount(flat_e, length=N_EXPERTS)
    padded = (counts + MOE_BLOCK - 1) // MOE_BLOCK * MOE_BLOCK
    start = jnp.cumsum(counts) - counts
    padded_end = jnp.cumsum(padded)
    padded_start = padded_end - padded
    dest = padded_start[sorted_e] + jnp.arange(n_assign) - start[sorted_e]
    n_rows = n_assign + N_EXPERTS * MOE_BLOCK
    n_blocks = n_rows // MOE_BLOCK
    row_tok = jnp.zeros((n_rows,), jnp.int32).at[dest].set(flat_tok[order])
    row_gate = jnp.zeros((n_rows,), jnp.float32).at[dest].set(flat_gate[order])
    block_start = jnp.arange(n_blocks) * MOE_BLOCK
    block_expert = jnp.minimum(jnp.sum(block_start[:, None] >= padded_end[None, :], axis=1), N_EXPERTS - 1)
    xs = hf[row_tok].reshape(n_blocks, MOE_BLOCK, D)

    def expert_block(args):
        xb, e = args
        return (jax.nn.silu(xb @ w_gate[e]) * (xb @ w_up[e])) @ w_down[e]

    ys = lax.map(expert_block, (xs, block_expert)).reshape(n_rows, D)
    out = jax.ops.segment_sum(ys * row_gate[:, None].astype(ys.dtype), row_tok, num_segments=T)
    return out.reshape(B, S, D)


def setup_inputs(seed: int = 0) -> dict:
    key = jax.random.key(seed)
    ks = jax.random.split(key, 27)
    f32 = jnp.float32
    L = DEPTH

    def nrm(k, shape, scale):
        return jax.random.normal(k, shape, f32) * scale

    dt = jnp.exp(jax.random.uniform(ks[8], (L, DN_HEADS), f32, math.log(1e-3), math.log(1e-1)))
    return {
        "x": nrm(ks[0], (BATCH, SEQ, D_MODEL), 1.0),
        "p": nrm(ks[1], (DEPTH, BATCH, SEQ, PLE_DIM), 1.0),
        "w_in": nrm(ks[2], (L, D_MODEL, IN_WIDTH), D_MODEL ** -0.5),
        "b_forget": 2.0 + nrm(ks[3], (L, ATTN_HEADS), 0.1),
        "pool_w": nrm(ks[4], (L, POOL_GROUPS, POOL_GROUP_DIM, POOL_GROUP_DIM), POOL_GROUP_DIM ** -0.5),
        "pool_scale": 1.0 + nrm(ks[5], (L, POOL_WIDTH), 0.1),
        "dn_conv": nrm(ks[6], (L, DN_CONV, 3 * DN_WIDTH), DN_CONV ** -0.5),
        "dn_a_log": jnp.log(jax.random.uniform(ks[7], (L, DN_HEADS), f32, 1.0, 16.0)),
        "dn_dt_bias": jnp.log(jnp.expm1(dt)),
        "dn_norm_w": 1.0 + nrm(ks[9], (L, DN_HEAD_DIM), 0.02),
        "w_br_attn": nrm(ks[10], (L, ATTN_WIDTH, D_MODEL), ATTN_WIDTH ** -0.5),
        "w_br_pool": nrm(ks[11], (L, POOL_WIDTH, D_MODEL), POOL_WIDTH ** -0.5),
        "w_br_dn": nrm(ks[12], (L, DN_WIDTH, D_MODEL), DN_WIDTH ** -0.5),
        "w_out": nrm(ks[13], (L, D_MODEL, D_MODEL), DEEPNORM_BETA * D_MODEL ** -0.5),
        "ln1_g": 1.0 + nrm(ks[14], (L, D_MODEL), 0.02),
        "ln1_b": nrm(ks[15], (L, D_MODEL), 0.02),
        "w_router_group": nrm(ks[16], (L, D_MODEL, N_GROUPS), D_MODEL ** -0.5),
        "b_router_group": nrm(ks[17], (L, N_GROUPS), 0.01),
        "w_router_expert": nrm(ks[18], (L, D_MODEL, N_EXPERTS), D_MODEL ** -0.5),
        "b_router_expert": nrm(ks[19], (L, N_EXPERTS), 0.01),
        "w_exp_gate": nrm(ks[20], (L, N_EXPERTS, D_MODEL, D_EXPERT), D_MODEL ** -0.5),
        "w_exp_up": nrm(ks[21], (L, N_EXPERTS, D_MODEL, D_EXPERT), D_MODEL ** -0.5),
        "w_exp_down": nrm(ks[22], (L, N_EXPERTS, D_EXPERT, D_MODEL), DEEPNORM_BETA * D_EXPERT ** -0.5),
        "w_ple_proj": nrm(ks[23], (L, PLE_DIM, D_MODEL), DEEPNORM_BETA * PLE_DIM ** -0.5),
        "w_ple_gate": nrm(ks[24], (L, D_MODEL, D_MODEL), D_MODEL ** -0.5),
        "ln2_g": 1.0 + nrm(ks[25], (L, D_MODEL), 0.02),
        "ln2_b": nrm(ks[26], (L, D_MODEL), 0.02),
    }


def reference(x, p, w_in, b_forget, pool_w, pool_scale, dn_conv, dn_a_log, dn_dt_bias, dn_norm_w,
              w_br_attn, w_br_pool, w_br_dn, w_out, ln1_g, ln1_b, w_router_group, b_router_group,
              w_router_expert, b_router_expert, w_exp_gate, w_exp_up, w_exp_down, w_ple_proj,
              w_ple_gate, ln2_g, ln2_b):
    for i in range(DEPTH):
        mix = token_mixer(x, w_in[i], b_forget[i], pool_w[i], pool_scale[i], dn_conv[i], dn_a_log[i],
                          dn_dt_bias[i], dn_norm_w[i], w_br_attn[i], w_br_pool[i], w_br_dn[i], w_out[i])
        x = layer_norm(DEEPNORM_ALPHA * x + mix, ln1_g[i], ln1_b[i])
        moe = hierarchical_moe(x, w_router_group[i], b_router_group[i], w_router_expert[i],
                               b_router_expert[i], w_exp_gate[i], w_exp_up[i], w_exp_down[i])
        ple = jax.nn.sigmoid(x @ w_ple_gate[i]) * (p[i] @ w_ple_proj[i])
        x = layer_norm(DEEPNORM_ALPHA * x + moe + ple, ln2_g[i], ln2_b[i])
    return x
```

```python
import functools

import jax
import jax.numpy as jnp
from jax import lax
from jax.experimental import pallas as pl
from jax.experimental.pallas import tpu as pltpu

F32 = jnp.float32
MXU_DTYPE = jnp.bfloat16

D_MODEL = 1024
DEPTH = 4
PLE_DIM = 256
ATTN_HEADS = 8
ATTN_HEAD_DIM = 64
ATTN_WIDTH = ATTN_HEADS * ATTN_HEAD_DIM
POOL_WINDOWS = (2, 4, 8, 16)
POOL_GROUP_DIM = 128
POOL_WIDTH = len(POOL_WINDOWS) * POOL_GROUP_DIM
DN_HEADS = 4
DN_HEAD_DIM = 128
DN_WIDTH = DN_HEADS * DN_HEAD_DIM
DN_CONV = 4
N_GROUPS = 4
EXPERTS_PER_GROUP = 8
N_EXPERTS = N_GROUPS * EXPERTS_PER_GROUP
D_EXPERT = 512
DEEPNORM_ALPHA = (2 * DEPTH) ** 0.25
LN_EPS = 1e-5
NORM_EPS = 1e-6
NEG_INF = -1e30

LANES = 128
SUBLANES = 8
VMEM_LIMIT = 56 * 1024 * 1024

FORGET_LANE = 0
DECAY_LANE = ATTN_HEADS
BETA_LANE = ATTN_HEADS + DN_HEADS
GATE_ROWS = 16

TM_PROJ = 256
TQ_ATTN = 512
TS_POOL = 512
DN_CHUNK = 128
TM_MERGE = 256
TM_ROWS = 256
MOE_BLOCK = 256
CUM_ROWS = 256


def _dot(a, b):
    return jnp.dot(a, b, preferred_element_type=F32)


def _dot_nt(a, b):
    return lax.dot_general(a, b, (((1,), (1,)), ((), ())), preferred_element_type=F32)


def _dot_exact(a, b):
    return jnp.dot(a, b, preferred_element_type=F32, precision=lax.Precision.HIGHEST)


def _mx(a):
    return a.astype(MXU_DTYPE)


def _sigmoid(z):
    return 1.0 / (1.0 + jnp.exp(-z))


def _silu(z):
    return z * _sigmoid(z)


def _layer_norm(h, g, b):
    mu = jnp.mean(h, axis=-1, keepdims=True)
    hc = h - mu
    var = jnp.mean(hc * hc, axis=-1, keepdims=True)
    return hc * lax.rsqrt(var + LN_EPS) * g + b


def _const_spec(shape):
    zeros = (0,) * len(shape)
    return pl.BlockSpec(shape, lambda *_: zeros, pipeline_mode=pl.Buffered(1))


def _params(*sem):
    return pltpu.CompilerParams(dimension_semantics=sem, vmem_limit_bytes=VMEM_LIMIT)


def _inproj_kernel(x_ref, wa_ref, wp_ref, wd_ref, wg_ref, ws_ref,
                   qkv_ref, pu_ref, dqkv_ref, dg_ref, small_ref):
    xb = _mx(x_ref[...])
    qkv_ref[...] = _dot(xb, wa_ref[...]).astype(qkv_ref.dtype)
    pu_ref[...] = _dot(xb, wp_ref[...])
    dqkv_ref[...] = _dot(xb, wd_ref[...])
    dg_ref[...] = _dot(xb, wg_ref[...])
    small_ref[...] = _dot(xb, ws_ref[...])


def _inproj(x2, wa, wp, wd, wg, ws):
    T = x2.shape[0]
    tm = TM_PROJ
    row = lambda n: pl.BlockSpec((tm, n), lambda i: (i, 0))
    return pl.pallas_call(
        _inproj_kernel,
        grid=(T // tm,),
        in_specs=[row(D_MODEL), _const_spec(wa.shape), _const_spec(wp.shape), _const_spec(wd.shape),
                  _const_spec(wg.shape), _const_spec(ws.shape)],
        out_specs=[row(3 * ATTN_WIDTH), row(POOL_WIDTH), row(3 * DN_WIDTH), row(DN_WIDTH), row(LANES)],
        out_shape=[jax.ShapeDtypeStruct((T, 3 * ATTN_WIDTH), MXU_DTYPE),
                   jax.ShapeDtypeStruct((T, POOL_WIDTH), F32),
                   jax.ShapeDtypeStruct((T, 3 * DN_WIDTH), F32),
                   jax.ShapeDtypeStruct((T, DN_WIDTH), F32),
                   jax.ShapeDtypeStruct((T, LANES), F32)],
        compiler_params=_params("parallel"),
        name="inproj",
    )(x2, wa, wp, wd, wg, ws)


def _gates_kernel(small_ref, bias_ref, alog_ref, g_ref, gt_ref):
    S = small_ref.shape[1]
    R = CUM_ROWS
    ri = lax.broadcasted_iota(jnp.int32, (R, R), 0)
    ci = lax.broadcasted_iota(jnp.int32, (R, R), 1)
    tri_full = (ri >= ci).astype(F32)
    tri_chunk = ((ri >= ci) & (ri // DN_CHUNK == ci // DN_CHUNK)).astype(F32)
    lane = lax.broadcasted_iota(jnp.int32, (R, LANES), 1)
    is_forget = lane < DECAY_LANE
    is_decay = (lane >= DECAY_LANE) & (lane < BETA_LANE)
    carry = jnp.zeros((1, LANES), F32)
    for r in range(S // R):
        z = small_ref[0, r * R:(r + 1) * R, :] + bias_ref[...]
        soft = jnp.log(1.0 + jnp.exp(-jnp.abs(z)))
        log_f = jnp.minimum(z, 0.0) - soft
        decay = -jnp.exp(alog_ref[...]) * (jnp.maximum(z, 0.0) + soft)
        vals = jnp.where(is_forget, log_f, jnp.where(is_decay, decay, _sigmoid(z)))
        cum_full = _dot_exact(tri_full, vals) + carry
        cum_chunk = _dot_exact(tri_chunk, vals)
        carry = cum_full[R - 1:R, :]
        out = jnp.where(is_forget, cum_full, jnp.where(is_decay, cum_chunk, vals))
        g_ref[0, r * R:(r + 1) * R, :] = out
        gt_ref[0, :, r * R:(r + 1) * R] = out.T[:GATE_ROWS, :]


def _gates(small3, bias_row, alog_row):
    B, S, _ = small3.shape
    return pl.pallas_call(
        _gates_kernel,
        grid=(B,),
        in_specs=[pl.BlockSpec((1, S, LANES), lambda b: (b, 0, 0)),
                  pl.BlockSpec((1, LANES), lambda b: (0, 0)),
                  pl.BlockSpec((1, LANES), lambda b: (0, 0))],
        out_specs=[pl.BlockSpec((1, S, LANES), lambda b: (b, 0, 0)),
                   pl.BlockSpec((1, GATE_ROWS, S), lambda b: (b, 0, 0))],
        out_shape=[jax.ShapeDtypeStruct((B, S, LANES), F32),
                   jax.ShapeDtypeStruct((B, GATE_ROWS, S), F32)],
        compiler_params=_params("parallel"),
        name="gates",
    )(small3, bias_row, alog_row)


def _fox_kernel(q_ref, k_ref, v_ref, cq_ref, ck_ref, o_ref, m_sc, l_sc, acc_sc):
    i = pl.program_id(1)
    j = pl.program_id(2)
    tq = q_ref.shape[1]
    tk = k_ref.shape[1]

    @pl.when(j == 0)
    def _():
        m_sc[...] = jnp.full(m_sc.shape, NEG_INF, F32)
        l_sc[...] = jnp.zeros(l_sc.shape, F32)
        acc_sc[...] = jnp.zeros(acc_sc.shape, F32)

    def sweep(masked):
        lane = lax.broadcasted_iota(jnp.int32, (1, LANES), 1)
        if masked:
            rows = lax.broadcasted_iota(jnp.int32, (tq, tk), 0)
            cols = lax.broadcasted_iota(jnp.int32, (tq, tk), 1)
            keep = cols <= rows
        for pair in range(ATTN_HEADS // 2):
            sl = slice(pair * LANES, (pair + 1) * LANES)
            q = q_ref[0, :, sl] * (ATTN_HEAD_DIM ** -0.5)
            k = k_ref[0, :, sl]
            v = v_ref[0, :, sl]
            for half in range(2):
                h = 2 * pair + half
                mine = (lane >= half * ATTN_HEAD_DIM) & (lane < (half + 1) * ATTN_HEAD_DIM)
                qh = jnp.where(mine, q, jnp.zeros_like(q))
                s = _dot_nt(qh, k)
                s = s + cq_ref[0, :, h:h + 1] - ck_ref[0, h:h + 1, :]
                if masked:
                    s = jnp.where(keep, s, NEG_INF)
                m_old = m_sc[h]
                m_new = jnp.maximum(m_old, jnp.max(s, axis=-1, keepdims=True))
                p = jnp.exp(s - m_new)
                alpha = jnp.exp(m_old - m_new)
                l_sc[h] = alpha * l_sc[h] + jnp.sum(p, axis=-1, keepdims=True)
                acc_sc[h] = alpha * acc_sc[h] + _dot(_mx(p), v)
                m_sc[h] = m_new

    @pl.when(j < i)
    def _():
        sweep(False)

    @pl.when(j == i)
    def _():
        sweep(True)
        lane = lax.broadcasted_iota(jnp.int32, (1, LANES), 1)
        for pair in range(ATTN_HEADS // 2):
            lo = acc_sc[2 * pair] / l_sc[2 * pair]
            hi = acc_sc[2 * pair + 1] / l_sc[2 * pair + 1]
            out = jnp.where(lane < ATTN_HEAD_DIM, lo, hi)
            o_ref[0, :, pair * LANES:(pair + 1) * LANES] = out.astype(o_ref.dtype)


def _fox(qkv3, gates, gates_t):
    B, S, _ = qkv3.shape
    t = TQ_ATTN
    n = S // t
    kv_row = lambda b, i, j: jnp.minimum(j, i)
    return pl.pallas_call(
        _fox_kernel,
        grid=(B, n, n),
        in_specs=[pl.BlockSpec((1, t, ATTN_WIDTH), lambda b, i, j: (b, i, 0)),
                  pl.BlockSpec((1, t, ATTN_WIDTH), lambda b, i, j: (b, kv_row(b, i, j), 1)),
                  pl.BlockSpec((1, t, ATTN_WIDTH), lambda b, i, j: (b, kv_row(b, i, j), 2)),
                  pl.BlockSpec((1, t, LANES), lambda b, i, j: (b, i, 0)),
                  pl.BlockSpec((1, GATE_ROWS, t), lambda b, i, j: (b, 0, kv_row(b, i, j)))],
        out_specs=pl.BlockSpec((1, t, ATTN_WIDTH), lambda b, i, j: (b, i, 0)),
        out_shape=jax.ShapeDtypeStruct((B, S, ATTN_WIDTH), MXU_DTYPE),
        scratch_shapes=[pltpu.VMEM((ATTN_HEADS, t, 1), F32),
                        pltpu.VMEM((ATTN_HEADS, t, 1), F32),
                        pltpu.VMEM((ATTN_HEADS, t, LANES), F32)],
        compiler_params=_params("parallel", "parallel", "arbitrary"),
        name="fox_attention",
    )(qkv3, qkv3, qkv3, gates, gates_t)


POOL_HALO = 16


def _pool_kernel(u_ref, w_ref, scale_ref, y_ref, halo_sc):
    s_idx = pl.program_id(1)
    ts = u_ref.shape[1]

    @pl.when(s_idx == 0)
    def _():
        halo_sc[...] = jnp.zeros(halo_sc.shape, F32)

    u = u_ref[0]
    ext = jnp.concatenate([halo_sc[...], u], axis=0)
    halo_sc[...] = u[ts - POOL_HALO:, :]
    pos = (lax.broadcasted_iota(jnp.int32, (ts, 1), 0) + s_idx * ts + 1).astype(F32)
    for g, w in enumerate(POOL_WINDOWS):
        sl = slice(g * POOL_GROUP_DIM, (g + 1) * POOL_GROUP_DIM)
        acc = ext[:, sl]
        span = 1
        while span < w:
            acc = acc + pltpu.roll(acc, span, axis=0)
            span *= 2
        mean = acc[POOL_HALO:, :] / jnp.minimum(pos, float(w))
        d = mean - u[:, sl]
        y = _dot(_mx(d), w_ref[g]) * scale_ref[:, sl]
        y_ref[0, :, sl] = y.astype(y_ref.dtype)


def _pool(pu3, pool_w, pool_scale_row):
    B, S, _ = pu3.shape
    ts = TS_POOL
    return pl.pallas_call(
        _pool_kernel,
        grid=(B, S // ts),
        in_specs=[pl.BlockSpec((1, ts, POOL_WIDTH), lambda b, s: (b, s, 0)),
                  pl.BlockSpec(pool_w.shape, lambda b, s: (0, 0, 0)),
                  pl.BlockSpec((1, POOL_WIDTH), lambda b, s: (0, 0))],
        out_specs=pl.BlockSpec((1, ts, POOL_WIDTH), lambda b, s: (b, s, 0)),
        out_shape=jax.ShapeDtypeStruct((B, S, POOL_WIDTH), MXU_DTYPE),
        scratch_shapes=[pltpu.VMEM((POOL_HALO, POOL_WIDTH), F32)],
        compiler_params=_params("parallel", "arbitrary"),
        name="pool",
    )(pu3, pool_w, pool_scale_row)


DN_HALO = SUBLANES


def _unit_lower_inverse(low):
    n = low.shape[0]
    eye = (lax.broadcasted_iota(jnp.int32, (n, n), 0) == lax.broadcasted_iota(jnp.int32, (n, n), 1)).astype(F32)
    inv = eye - low
    power = _dot(_mx(low), _mx(low))
    span = 2
    while True:
        inv = inv + _dot(_mx(inv), _mx(power))
        span *= 2
        if span >= n:
            return inv
        power = _dot(_mx(power), _mx(power))


def _dn_kernel(qkv_ref, dg_ref, g_ref, gt_ref, cw_ref, nw_ref, o_ref, halo_sc, state_sc):
    c_idx = pl.program_id(1)
    C = qkv_ref.shape[1]

    @pl.when(c_idx == 0)
    def _():
        halo_sc[...] = jnp.zeros(halo_sc.shape, F32)
        state_sc[...] = jnp.zeros(state_sc.shape, F32)

    u = qkv_ref[0]
    ext = jnp.concatenate([halo_sc[...], u], axis=0)
    halo_sc[...] = u[C - DN_HALO:, :]
    conv = u * cw_ref[DN_CONV - 1:DN_CONV, :]
    for d in range(1, DN_CONV):
        conv = conv + pltpu.roll(ext, d, axis=0)[DN_HALO:, :] * cw_ref[DN_CONV - 1 - d:DN_CONV - d, :]
    act = _silu(conv)

    ri = lax.broadcasted_iota(jnp.int32, (C, C), 0)
    ci = lax.broadcasted_iota(jnp.int32, (C, C), 1)
    causal = ri >= ci
    strict = ri > ci
    for h in range(DN_HEADS):
        sl = slice(h * DN_HEAD_DIM, (h + 1) * DN_HEAD_DIM)
        q = act[:, sl]
        k = act[:, DN_WIDTH + h * DN_HEAD_DIM:DN_WIDTH + (h + 1) * DN_HEAD_DIM]
        v = act[:, 2 * DN_WIDTH + h * DN_HEAD_DIM:2 * DN_WIDTH + (h + 1) * DN_HEAD_DIM]
        q = q * lax.rsqrt(jnp.sum(q * q, axis=-1, keepdims=True) + NORM_EPS) * (DN_HEAD_DIM ** -0.5)
        k = k * lax.rsqrt(jnp.sum(k * k, axis=-1, keepdims=True) + NORM_EPS)
        gc = g_ref[0, :, DECAY_LANE + h:DECAY_LANE + h + 1]
        beta = g_ref[0, :, BETA_LANE + h:BETA_LANE + h + 1]
        gc_row = gt_ref[0, DECAY_LANE + h:DECAY_LANE + h + 1, :]
        g_last = gc[C - 1:C, :]
        decay = jnp.where(causal, jnp.exp(jnp.where(causal, gc - gc_row, 0.0)), 0.0)
        kb = _mx(k)
        low = jnp.where(strict, beta * _dot_nt(kb, kb) * decay, 0.0)
        inv = _unit_lower_inverse(low)
        e_gc = jnp.exp(gc)
        rhs = jnp.concatenate([v * beta, k * (beta * e_gc)], axis=1)
        sol = _dot(_mx(inv), _mx(rhs))
        u_intra = sol[:, :DN_HEAD_DIM]
        w_state = sol[:, DN_HEAD_DIM:]
        attn = jnp.where(causal, _dot_nt(_mx(q), kb) * decay, 0.0)
        q_dec = q * e_gc
        k_dec = k * jnp.exp(g_last - gc)

        state = state_sc[h]
        sb = _mx(state)
        u_new = u_intra - _dot(_mx(w_state), sb)
        ub = _mx(u_new)
        o = _dot(_mx(q_dec), sb) + _dot(_mx(attn), ub)
        state_sc[h] = state * jnp.exp(g_last) + _dot(_mx(k_dec.T), ub)

        o = o * lax.rsqrt(jnp.mean(o * o, axis=-1, keepdims=True) + NORM_EPS) * nw_ref[...]
        o = o * _silu(dg_ref[0, :, sl])
        o_ref[0, :, sl] = o.astype(o_ref.dtype)


def _deltanet(dqkv3, dg3, gates, gates_t, conv_w, norm_w_row):
    B, S, _ = dqkv3.shape
    C = DN_CHUNK
    return pl.pallas_call(
        _dn_kernel,
        grid=(B, S // C),
        in_specs=[pl.BlockSpec((1, C, 3 * DN_WIDTH), lambda b, c: (b, c, 0)),
                  pl.BlockSpec((1, C, DN_WIDTH), lambda b, c: (b, c, 0)),
                  pl.BlockSpec((1, C, LANES), lambda b, c: (b, c, 0)),
                  pl.BlockSpec((1, GATE_ROWS, C), lambda b, c: (b, 0, c)),
                  pl.BlockSpec((DN_CONV, 3 * DN_WIDTH), lambda b, c: (0, 0)),
                  pl.BlockSpec((1, DN_HEAD_DIM), lambda b, c: (0, 0))],
        out_specs=pl.BlockSpec((1, C, DN_WIDTH), lambda b, c: (b, c, 0)),
        out_shape=jax.ShapeDtypeStruct((B, S, DN_WIDTH), MXU_DTYPE),
        scratch_shapes=[pltpu.VMEM((DN_HALO, 3 * DN_WIDTH), F32),
                        pltpu.VMEM((DN_HEADS, DN_HEAD_DIM, DN_HEAD_DIM), F32)],
        compiler_params=_params("parallel", "arbitrary"),
        name="deltanet",
    )(dqkv3, dg3, gates, gates_t, conv_w, norm_w_row)


ROUTE_E1, ROUTE_E2, ROUTE_R1, ROUTE_R2, ROUTE_G1, ROUTE_G2 = range(6)
ROUTER_EXPERT_LANE = N_GROUPS


def _merge_kernel(x_ref, ya_ref, yp_ref, yd_ref, p_ref, wgate_ref, wa_ref, wp_ref, wd_ref, wo_ref,
                  g1_ref, b1_ref, wr_ref, br_ref, wpg_ref, wpp_ref,
                  x1_ref, base_ref, route_ref, cnt_ref, cnt_sc):
    step = pl.program_id(0)
    tm = x_ref.shape[0]

    @pl.when(step == 0)
    def _():
        cnt_sc[...] = jnp.zeros(cnt_sc.shape, F32)

    x = x_ref[...]
    gate = _dot(_mx(x), wgate_ref[...])
    merged = (_sigmoid(gate[:, :D_MODEL]) * _dot(ya_ref[...], wa_ref[...])
              + _sigmoid(gate[:, D_MODEL:2 * D_MODEL]) * _dot(yp_ref[...], wp_ref[...])
              + _sigmoid(gate[:, 2 * D_MODEL:]) * _dot(yd_ref[...], wd_ref[...]))
    mix = _dot(_mx(merged), wo_ref[...])
    x1 = _layer_norm(DEEPNORM_ALPHA * x + mix, g1_ref[...], b1_ref[...])
    x1_ref[...] = x1
    x1b = _mx(x1)
    ple = _sigmoid(_dot(x1b, wpg_ref[...])) * _dot(_mx(p_ref[...]), wpp_ref[...])
    base_ref[...] = DEEPNORM_ALPHA * x1 + ple

    logits = _dot(x1b, wr_ref[...]) + br_ref[...]
    lane = lax.broadcasted_iota(jnp.int32, (tm, LANES), 1)
    ninf = -jnp.inf

    def first_argmax(vals):
        top = jnp.max(vals, axis=-1, keepdims=True)
        idx = jnp.min(jnp.where(vals == top, lane, LANES), axis=-1, keepdims=True)
        return top, idx

    group_logits = jnp.where(lane < N_GROUPS, logits, ninf)
    g_top, g_idx = first_argmax(group_logits)
    p_group = 1.0 / jnp.sum(jnp.exp(group_logits - g_top), axis=-1, keepdims=True)
    lo = ROUTER_EXPERT_LANE + EXPERTS_PER_GROUP * g_idx
    local = jnp.where((lane >= lo) & (lane < lo + EXPERTS_PER_GROUP), logits, ninf)
    top1, idx1 = first_argmax(local)
    top2, idx2 = first_argmax(jnp.where(lane == idx1, ninf, local))
    z = jnp.sum(jnp.exp(local - top1), axis=-1, keepdims=True)
    prob1 = 1.0 / z
    prob2 = jnp.exp(top2 - top1) / z
    gate1 = p_group * prob1 / (prob1 + prob2)
    gate2 = p_group * prob2 / (prob1 + prob2)
    e1 = idx1 - ROUTER_EXPERT_LANE
    e2 = idx2 - ROUTER_EXPERT_LANE

    onehot = ((lane == e1) | (lane == e2)).astype(F32)
    ri = lax.broadcasted_iota(jnp.int32, (tm, tm), 0)
    ci = lax.broadcasted_iota(jnp.int32, (tm, tm), 1)
    before = _dot((ri > ci).astype(jnp.bfloat16), onehot.astype(jnp.bfloat16)) + cnt_sc[...]
    rank1 = jnp.sum(jnp.where(lane == e1, before, 0.0), axis=-1, keepdims=True)
    rank2 = jnp.sum(jnp.where(lane == e2, before, 0.0), axis=-1, keepdims=True)
    cnt_sc[...] = cnt_sc[...] + jnp.sum(onehot, axis=0, keepdims=True)
    cnt_ref[...] = jnp.broadcast_to(cnt_sc[...], cnt_ref.shape)

    route = jnp.zeros((tm, LANES), F32)
    for pos, val in ((ROUTE_E1, e1.astype(F32)), (ROUTE_E2, e2.astype(F32)), (ROUTE_R1, rank1),
                     (ROUTE_R2, rank2), (ROUTE_G1, gate1), (ROUTE_G2, gate2)):
        route = jnp.where(lane == pos, val, route)
    route_ref[...] = route


def _merge(x2, ya, yp, yd, p2, wgate, wa, wp, wd, wo, g1, b1, wr, br, wpg, wpp):
    T = x2.shape[0]
    tm = TM_MERGE
    row = lambda n: pl.BlockSpec((tm, n), lambda i: (i, 0))
    consts = [wgate, wa, wp, wd, wo, g1, b1, wr, br, wpg, wpp]
    return pl.pallas_call(
        _merge_kernel,
        grid=(T // tm,),
        in_specs=[row(D_MODEL), row(ATTN_WIDTH), row(POOL_WIDTH), row(DN_WIDTH), row(PLE_DIM)]
                 + [_const_spec(c.shape) for c in consts],
        out_specs=[row(D_MODEL), row(D_MODEL), row(LANES), pl.BlockSpec((SUBLANES, LANES), lambda i: (0, 0))],
        out_shape=[jax.ShapeDtypeStruct((T, D_MODEL), F32),
                   jax.ShapeDtypeStruct((T, D_MODEL), F32),
                   jax.ShapeDtypeStruct((T, LANES), F32),
                   jax.ShapeDtypeStruct((SUBLANES, LANES), F32)],
        scratch_shapes=[pltpu.VMEM((1, LANES), F32)],
        compiler_params=_params("arbitrary"),
        name="merge_router",
    )(x2, ya, yp, yd, p2, *consts)


def _row_copy(src, src_row, dst, dst_row, sem):
    return pltpu.make_async_copy(src.at[pl.ds(src_row, 1), :], dst.at[pl.ds(dst_row, 1), :], sem)


def _dispatch_kernel(d1_ref, d2_ref, x_ref, xs_in_ref, xs_ref, sem):
    del xs_in_ref
    tm = x_ref.shape[0]

    def issue(t, carry):
        _row_copy(x_ref, t, xs_ref, d1_ref[0, 0, t], sem.at[0]).start()
        _row_copy(x_ref, t, xs_ref, d2_ref[0, 0, t], sem.at[1]).start()
        return carry

    lax.fori_loop(0, tm, issue, 0)

    def drain(t, carry):
        _row_copy(x_ref, t, xs_ref, d1_ref[0, 0, t], sem.at[0]).wait()
        _row_copy(x_ref, t, xs_ref, d2_ref[0, 0, t], sem.at[1]).wait()
        return carry

    lax.fori_loop(0, tm, drain, 0)


def _dispatch(x1, dest1, dest2, n_rows):
    T = x1.shape[0]
    tm = TM_ROWS
    idx_spec = pl.BlockSpec((1, 1, tm), lambda i: (i, 0, 0), memory_space=pltpu.SMEM)
    xs0 = jnp.zeros((n_rows, D_MODEL), F32)
    return pl.pallas_call(
        _dispatch_kernel,
        grid=(T // tm,),
        in_specs=[idx_spec, idx_spec, pl.BlockSpec((tm, D_MODEL), lambda i: (i, 0)),
                  pl.BlockSpec(memory_space=pl.ANY)],
        out_specs=pl.BlockSpec(memory_space=pl.ANY),
        out_shape=jax.ShapeDtypeStruct((n_rows, D_MODEL), F32),
        scratch_shapes=[pltpu.SemaphoreType.DMA((2,))],
        input_output_aliases={3: 0},
        compiler_params=_params("arbitrary"),
        name="moe_dispatch",
    )(dest1.reshape(T // tm, 1, tm), dest2.reshape(T // tm, 1, tm), x1, xs0)


def _experts_kernel(bexp_ref, nreal_ref, xs_ref, wg_ref, wu_ref, wd_ref, ys_ref):
    del bexp_ref
    i = pl.program_id(0)

    @pl.when(i < nreal_ref[0])
    def _():
        xb = _mx(xs_ref[...])
        hid = _silu(_dot(xb, wg_ref[0])) * _dot(xb, wu_ref[0])
        ys_ref[...] = _dot(_mx(hid), wd_ref[0])

    @pl.when(i >= nreal_ref[0])
    def _():
        ys_ref[...] = jnp.zeros(ys_ref.shape, F32)


def _experts(xs, block_expert, n_real, wg, wu, wd):
    n_rows = xs.shape[0]
    blk = MOE_BLOCK
    grid_spec = pltpu.PrefetchScalarGridSpec(
        num_scalar_prefetch=2,
        grid=(n_rows // blk,),
        in_specs=[pl.BlockSpec((blk, D_MODEL), lambda i, be, nr: (i, 0)),
                  pl.BlockSpec((1, D_MODEL, D_EXPERT), lambda i, be, nr: (be[i], 0, 0)),
                  pl.BlockSpec((1, D_MODEL, D_EXPERT), lambda i, be, nr: (be[i], 0, 0)),
                  pl.BlockSpec((1, D_EXPERT, D_MODEL), lambda i, be, nr: (be[i], 0, 0))],
        out_specs=pl.BlockSpec((blk, D_MODEL), lambda i, be, nr: (i, 0)),
    )
    return pl.pallas_call(
        _experts_kernel,
        grid_spec=grid_spec,
        out_shape=jax.ShapeDtypeStruct((n_rows, D_MODEL), F32),
        compiler_params=_params("arbitrary"),
        name="moe_experts",
    )(block_expert, n_real, xs, wg, wu, wd)


def _combine_kernel(d1_ref, d2_ref, route_ref, base_ref, ys_ref, g2_ref, b2_ref, o_ref, buf, sem):
    tm = base_ref.shape[0]

    def issue(t, carry):
        _row_copy(ys_ref, d1_ref[0, 0, t], buf.at[0], t, sem.at[0]).start()
        _row_copy(ys_ref, d2_ref[0, 0, t], buf.at[1], t, sem.at[1]).start()
        return carry

    lax.fori_loop(0, tm, issue, 0)

    def drain(t, carry):
        _row_copy(ys_ref, d1_ref[0, 0, t], buf.at[0], t, sem.at[0]).wait()
        _row_copy(ys_ref, d2_ref[0, 0, t], buf.at[1], t, sem.at[1]).wait()
        return carry

    lax.fori_loop(0, tm, drain, 0)

    gate1 = route_ref[:, ROUTE_G1:ROUTE_G1 + 1]
    gate2 = route_ref[:, ROUTE_G2:ROUTE_G2 + 1]
    moe = gate1 * buf[0] + gate2 * buf[1]
    o_ref[...] = _layer_norm(base_ref[...] + moe, g2_ref[...], b2_ref[...])


def _combine(dest1, dest2, route, base, ys, g2, b2):
    T = base.shape[0]
    tm = TM_ROWS
    idx_spec = pl.BlockSpec((1, 1, tm), lambda i: (i, 0, 0), memory_space=pltpu.SMEM)
    return pl.pallas_call(
        _combine_kernel,
        grid=(T // tm,),
        in_specs=[idx_spec, idx_spec,
                  pl.BlockSpec((tm, LANES), lambda i: (i, 0)),
                  pl.BlockSpec((tm, D_MODEL), lambda i: (i, 0)),
                  pl.BlockSpec(memory_space=pl.ANY),
                  pl.BlockSpec((1, D_MODEL), lambda i: (0, 0)),
                  pl.BlockSpec((1, D_MODEL), lambda i: (0, 0))],
        out_specs=pl.BlockSpec((tm, D_MODEL), lambda i: (i, 0)),
        out_shape=jax.ShapeDtypeStruct((T, D_MODEL), F32),
        scratch_shapes=[pltpu.VMEM((2, tm, D_MODEL), F32), pltpu.SemaphoreType.DMA((2,))],
        compiler_params=_params("arbitrary"),
        name="moe_combine",
    )(dest1.reshape(T // tm, 1, tm), dest2.reshape(T // tm, 1, tm), route, base, ys, g2, b2)


def _pad_lanes(a):
    return jnp.pad(a, ((0, 0), (0, LANES - a.shape[1])))


def _layer(x2, p2, B, S, w_in, b_forget, pool_w, pool_scale, dn_conv, dn_a_log, dn_dt_bias, dn_norm_w,
           w_br_attn, w_br_pool, w_br_dn, w_out, ln1_g, ln1_b, w_rg, b_rg, w_re, b_re,
           w_eg, w_eu, w_ed, w_pp, w_pg, ln2_g, ln2_b):
    T = B * S
    o = 0
    cols = {}
    for name, width in (("attn", 3 * ATTN_WIDTH), ("forget", ATTN_HEADS), ("pool", POOL_WIDTH),
                        ("dn", 3 * DN_WIDTH), ("decay", DN_HEADS), ("beta", DN_HEADS),
                        ("dgate", DN_WIDTH), ("merge", 3 * D_MODEL)):
        cols[name] = w_in[:, o:o + width]
        o += width
    w_small = _pad_lanes(jnp.concatenate([cols["forget"], cols["decay"], cols["beta"]], axis=1))
    qkv, pu, dqkv, dg, small = _inproj(x2, _mx(cols["attn"]), _mx(cols["pool"]), _mx(cols["dn"]),
                                       _mx(cols["dgate"]), _mx(w_small))

    bias_row = _pad_lanes(jnp.concatenate([b_forget, dn_dt_bias])[None, :])
    alog_row = _pad_lanes(jnp.concatenate([jnp.zeros((ATTN_HEADS,), F32), dn_a_log])[None, :])
    gates, gates_t = _gates(small.reshape(B, S, LANES), bias_row, alog_row)

    y_attn = _fox(qkv.reshape(B, S, 3 * ATTN_WIDTH), gates, gates_t)
    y_pool = _pool(pu.reshape(B, S, POOL_WIDTH), _mx(pool_w), pool_scale[None, :])
    y_dn = _deltanet(dqkv.reshape(B, S, 3 * DN_WIDTH), dg.reshape(B, S, DN_WIDTH), gates, gates_t,
                     dn_conv, dn_norm_w[None, :])

    w_router = _pad_lanes(jnp.concatenate([w_rg, w_re], axis=1))
    b_router = _pad_lanes(jnp.concatenate([b_rg, b_re])[None, :])
    x1, base, route, counts = _merge(
        x2, y_attn.reshape(T, ATTN_WIDTH), y_pool.reshape(T, POOL_WIDTH), y_dn.reshape(T, DN_WIDTH), p2,
        _mx(cols["merge"]), _mx(w_br_attn), _mx(w_br_pool), _mx(w_br_dn), _mx(w_out),
        ln1_g[None, :], ln1_b[None, :], _mx(w_router), b_router, _mx(w_pg), _mx(w_pp))

    blk = MOE_BLOCK
    n_rows = 2 * T + N_EXPERTS * blk
    cnt = counts[0, :N_EXPERTS].astype(jnp.int32)
    padded = (cnt + blk - 1) // blk * blk
    padded_end = jnp.cumsum(padded)
    padded_start = padded_end - padded
    e1 = route[:, ROUTE_E1].astype(jnp.int32)
    e2 = route[:, ROUTE_E2].astype(jnp.int32)
    dest1 = padded_start[e1] + route[:, ROUTE_R1].astype(jnp.int32)
    dest2 = padded_start[e2] + route[:, ROUTE_R2].astype(jnp.int32)
    block_start = jnp.arange(n_rows // blk, dtype=jnp.int32) * blk
    block_expert = jnp.minimum(jnp.sum(block_start[:, None] >= padded_end[None, :], axis=1),
                               N_EXPERTS - 1).astype(jnp.int32)
    n_real = (padded_end[-1:] // blk).astype(jnp.int32)

    xs = _dispatch(x1, dest1, dest2, n_rows)
    ys = _experts(xs, block_expert, n_real, _mx(w_eg), _mx(w_eu), _mx(w_ed))
    return _combine(dest1, dest2, route, base, ys, ln2_g[None, :], ln2_b[None, :])


def kernel(x, p, w_in, b_forget, pool_w, pool_scale, dn_conv, dn_a_log, dn_dt_bias, dn_norm_w, w_br_attn, w_br_pool, w_br_dn, w_out, ln1_g, ln1_b, w_router_group, b_router_group, w_router_expert, b_router_expert, w_exp_gate, w_exp_up, w_exp_down, w_ple_proj, w_ple_gate, ln2_g, ln2_b):
    B, S, _ = x.shape
    x2 = x.reshape(B * S, D_MODEL)
    for i in range(DEPTH):
        x2 = _layer(x2, p[i].reshape(B * S, PLE_DIM), B, S, w_in[i], b_forget[i], pool_w[i], pool_scale[i],
                    dn_conv[i], dn_a_log[i], dn_dt_bias[i], dn_norm_w[i], w_br_attn[i], w_br_pool[i],
                    w_br_dn[i], w_out[i], ln1_g[i], ln1_b[i], w_router_group[i], b_router_group[i],
                    w_router_expert[i], b_router_expert[i], w_exp_gate[i], w_exp_up[i], w_exp_down[i],
                    w_ple_proj[i], w_ple_gate[i], ln2_g[i], ln2_b[i])
    return x2.reshape(B, S, D_MODEL)
```

```python
import functools

import numpy as np
import jax
import jax.numpy as jnp
from jax import lax
from jax.experimental import pallas as pl
from jax.experimental.pallas import tpu as pltpu

F32 = jnp.float32
MXU_DTYPE = jnp.bfloat16

D_MODEL = 1024
DEPTH = 4
PLE_DIM = 256
ATTN_HEADS = 8
ATTN_HEAD_DIM = 64
ATTN_WIDTH = ATTN_HEADS * ATTN_HEAD_DIM
POOL_WINDOWS = (2, 4, 8, 16)
POOL_GROUP_DIM = 128
POOL_WIDTH = len(POOL_WINDOWS) * POOL_GROUP_DIM
DN_HEADS = 4
DN_HEAD_DIM = 128
DN_WIDTH = DN_HEADS * DN_HEAD_DIM
DN_CONV = 4
N_GROUPS = 4
EXPERTS_PER_GROUP = 8
N_EXPERTS = N_GROUPS * EXPERTS_PER_GROUP
D_EXPERT = 512
DEEPNORM_ALPHA = (2 * DEPTH) ** 0.25
LN_EPS = 1e-5
NORM_EPS = 1e-6
NEG_INF = -1e30

LANES = 128
SUBLANES = 8
VMEM_LIMIT = 56 * 1024 * 1024

FORGET_LANE = 0
DECAY_LANE = ATTN_HEADS
BETA_LANE = ATTN_HEADS + DN_HEADS
GATE_ROWS = 16

TM_PROJ = 256
TQ_ATTN = 512
TS_POOL = 512
DN_CHUNK = 128
TM_MERGE = 256
TM_ROWS = 256
MOE_BLOCK = 256
CUM_ROWS = 256


def _dot(a, b):
    return jnp.dot(a, b, preferred_element_type=F32)


def _dot_nt(a, b):
    return lax.dot_general(a, b, (((1,), (1,)), ((), ())), preferred_element_type=F32)


def _dot_exact(a, b):
    return jnp.dot(a, b, preferred_element_type=F32, precision=lax.Precision.HIGHEST)


def _mx(a):
    return a.astype(MXU_DTYPE)


def _sigmoid(z):
    return 1.0 / (1.0 + jnp.exp(-z))


def _silu(z):
    return z * _sigmoid(z)


def _layer_norm(h, g, b):
    mu = jnp.mean(h, axis=-1, keepdims=True)
    hc = h - mu
    var = jnp.mean(hc * hc, axis=-1, keepdims=True)
    return hc * lax.rsqrt(var + LN_EPS) * g + b


def _const_spec(shape):
    zeros = (0,) * len(shape)
    return pl.BlockSpec(shape, lambda *_: zeros, pipeline_mode=pl.Buffered(1))


def _params(*sem):
    return pltpu.CompilerParams(dimension_semantics=sem, vmem_limit_bytes=VMEM_LIMIT)


def _inproj_kernel(x_ref, wa_ref, wp_ref, wd_ref, wg_ref, ws_ref,
                   qkv_ref, pu_ref, dqkv_ref, dg_ref, small_ref):
    xb = _mx(x_ref[...])
    qkv_ref[...] = _dot(xb, wa_ref[...]).astype(qkv_ref.dtype)
    pu_ref[...] = _dot(xb, wp_ref[...])
    dqkv_ref[...] = _dot(xb, wd_ref[...])
    dg_ref[...] = _dot(xb, wg_ref[...])
    small_ref[...] = _dot(xb, ws_ref[...])


def _inproj(x2, wa, wp, wd, wg, ws):
    T = x2.shape[0]
    tm = TM_PROJ
    row = lambda n: pl.BlockSpec((tm, n), lambda i: (i, 0))
    return pl.pallas_call(
        _inproj_kernel,
        grid=(T // tm,),
        in_specs=[row(D_MODEL), _const_spec(wa.shape), _const_spec(wp.shape), _const_spec(wd.shape),
                  _const_spec(wg.shape), _const_spec(ws.shape)],
        out_specs=[row(3 * ATTN_WIDTH), row(POOL_WIDTH), row(3 * DN_WIDTH), row(DN_WIDTH), row(LANES)],
        out_shape=[jax.ShapeDtypeStruct((T, 3 * ATTN_WIDTH), MXU_DTYPE),
                   jax.ShapeDtypeStruct((T, POOL_WIDTH), F32),
                   jax.ShapeDtypeStruct((T, 3 * DN_WIDTH), F32),
                   jax.ShapeDtypeStruct((T, DN_WIDTH), F32),
                   jax.ShapeDtypeStruct((T, LANES), F32)],
        compiler_params=_params("parallel"),
        name="inproj",
    )(x2, wa, wp, wd, wg, ws)


BIAS_SPLIT_LANES = 3 * ATTN_HEADS
BIAS_ONES_LANE = BIAS_SPLIT_LANES


def _bias_placement():
    pq = np.zeros((LANES, ATTN_HEADS * LANES), np.float32)
    pk = np.zeros((LANES, ATTN_HEADS * LANES), np.float32)
    for h in range(ATTN_HEADS):
        base = h * LANES + (1 - h % 2) * ATTN_HEAD_DIM
        for part in range(3):
            pq[part * ATTN_HEADS + h, base + part] = 1.0
            pq[BIAS_ONES_LANE, base + 3 + part] = 1.0
            pk[BIAS_ONES_LANE, base + part] = 1.0
            pk[part * ATTN_HEADS + h, base + 3 + part] = -1.0
    return pq, pk


def _gates_kernel(small_ref, bias_ref, alog_ref, qkv_ref, pq_ref, pk_ref,
                  g_ref, gt_ref, qa_ref, ka_ref, v4_ref, carry_sc):
    R = small_ref.shape[1]

    @pl.when(pl.program_id(1) == 0)
    def _():
        carry_sc[...] = jnp.zeros(carry_sc.shape, F32)

    ri = lax.broadcasted_iota(jnp.int32, (R, R), 0)
    ci = lax.broadcasted_iota(jnp.int32, (R, R), 1)
    tri_full = (ri >= ci).astype(F32)
    tri_chunk = ((ri >= ci) & (ri // DN_CHUNK == ci // DN_CHUNK)).astype(F32)
    lane = lax.broadcasted_iota(jnp.int32, (R, LANES), 1)
    is_forget = lane < DECAY_LANE
    is_decay = (lane >= DECAY_LANE) & (lane < BETA_LANE)

    z = small_ref[0] + bias_ref[...]
    soft = jnp.log(1.0 + jnp.exp(-jnp.abs(z)))
    log_f = jnp.minimum(z, 0.0) - soft
    decay = -jnp.exp(alog_ref[...]) * (jnp.maximum(z, 0.0) + soft)
    vals = jnp.where(is_forget, log_f, jnp.where(is_decay, decay, _sigmoid(z)))
    cum_full = _dot_exact(tri_full, vals) + carry_sc[...]
    cum_chunk = _dot_exact(tri_chunk, vals)
    carry_sc[...] = cum_full[R - 1:R, :]
    out = jnp.where(is_forget, cum_full, jnp.where(is_decay, cum_chunk, vals))
    g_ref[0] = out
    gt_ref[0] = out.T[:GATE_ROWS, :]

    c = jnp.where(is_forget, out, 0.0)
    hi = _mx(c).astype(F32)
    mid = _mx(c - hi).astype(F32)
    lo = _mx(c - hi - mid).astype(F32)
    split = (hi + pltpu.roll(mid, ATTN_HEADS, axis=1) + pltpu.roll(lo, 2 * ATTN_HEADS, axis=1)
             + jnp.where(lane == BIAS_ONES_LANE, 1.0, 0.0))
    split_b = _mx(split)
    extra_q = _dot(split_b, pq_ref[...]).astype(qa_ref.dtype)
    extra_k = _dot(split_b, pk_ref[...]).astype(ka_ref.dtype)
    lane1 = lax.broadcasted_iota(jnp.int32, (1, LANES), 1)
    for pair in range(ATTN_HEADS // 2):
        qp = qkv_ref[0, :, pair * LANES:(pair + 1) * LANES] * (ATTN_HEAD_DIM ** -0.5)
        kp = qkv_ref[0, :, ATTN_WIDTH + pair * LANES:ATTN_WIDTH + (pair + 1) * LANES]
        v4_ref[0, pair] = qkv_ref[0, :, 2 * ATTN_WIDTH + pair * LANES:2 * ATTN_WIDTH + (pair + 1) * LANES]
        for half in range(2):
            h = 2 * pair + half
            own = (lane1 >= half * ATTN_HEAD_DIM) & (lane1 < (half + 1) * ATTN_HEAD_DIM)
            qa_ref[0, h] = jnp.where(own, qp, extra_q[:, h * LANES:(h + 1) * LANES])
            ka_ref[0, h] = jnp.where(own, kp, extra_k[:, h * LANES:(h + 1) * LANES])


def _gates(small3, bias_row, alog_row, qkv3):
    B, S, _ = small3.shape
    R = CUM_ROWS
    pq, pk = _bias_placement()
    pq = jnp.asarray(pq, MXU_DTYPE)
    pk = jnp.asarray(pk, MXU_DTYPE)
    H = ATTN_HEADS
    return pl.pallas_call(
        _gates_kernel,
        grid=(B, S // R),
        in_specs=[pl.BlockSpec((1, R, LANES), lambda b, r: (b, r, 0)),
                  pl.BlockSpec((1, LANES), lambda b, r: (0, 0)),
                  pl.BlockSpec((1, LANES), lambda b, r: (0, 0)),
                  pl.BlockSpec((1, R, 3 * ATTN_WIDTH), lambda b, r: (b, r, 0)),
                  pl.BlockSpec(pq.shape, lambda b, r: (0, 0)),
                  pl.BlockSpec(pk.shape, lambda b, r: (0, 0))],
        out_specs=[pl.BlockSpec((1, R, LANES), lambda b, r: (b, r, 0)),
                   pl.BlockSpec((1, GATE_ROWS, R), lambda b, r: (b, 0, r)),
                   pl.BlockSpec((1, H, R, LANES), lambda b, r: (b, 0, r, 0)),
                   pl.BlockSpec((1, H, R, LANES), lambda b, r: (b, 0, r, 0)),
                   pl.BlockSpec((1, H // 2, R, LANES), lambda b, r: (b, 0, r, 0))],
        out_shape=[jax.ShapeDtypeStruct((B, S, LANES), F32),
                   jax.ShapeDtypeStruct((B, GATE_ROWS, S), F32),
                   jax.ShapeDtypeStruct((B, H, S, LANES), MXU_DTYPE),
                   jax.ShapeDtypeStruct((B, H, S, LANES), MXU_DTYPE),
                   jax.ShapeDtypeStruct((B, H // 2, S, LANES), MXU_DTYPE)],
        scratch_shapes=[pltpu.VMEM((1, LANES), F32)],
        compiler_params=_params("parallel", "arbitrary"),
        name="gates",
    )(small3, bias_row, alog_row, qkv3, pq, pk)


def _fox_kernel(q_ref, k_ref, v_ref, o_ref, m_sc, l_sc, acc_sc):
    i = pl.program_id(1)
    j = pl.program_id(2)
    tq = q_ref.shape[2]
    tk = k_ref.shape[2]
    reps = tk // LANES

    @pl.when(j == 0)
    def _():
        m_sc[...] = jnp.full(m_sc.shape, NEG_INF, F32)
        l_sc[...] = jnp.zeros(l_sc.shape, F32)
        acc_sc[...] = jnp.zeros(acc_sc.shape, F32)

    def sweep(masked):
        if masked:
            rows = lax.broadcasted_iota(jnp.int32, (tq, tk), 0)
            cols = lax.broadcasted_iota(jnp.int32, (tq, tk), 1)
            keep = cols <= rows
        for h in range(ATTN_HEADS):
            s = _dot_nt(q_ref[0, h], k_ref[0, h])
            if masked:
                s = jnp.where(keep, s, NEG_INF)
            m_prev = m_sc[h]
            m_next = jnp.maximum(m_prev, jnp.max(s, axis=1)[:, None])
            p = jnp.exp(s - jnp.concatenate([m_next] * reps, axis=1))
            alpha = jnp.exp(m_prev - m_next)
            l_sc[h] = alpha * l_sc[h] + jnp.sum(p, axis=1)[:, None]
            m_sc[h] = m_next
            acc_sc[h] = alpha * acc_sc[h] + _dot(_mx(p), v_ref[0, h // 2])

    @pl.when(j < i)
    def _():
        sweep(False)

    @pl.when(j == i)
    def _():
        sweep(True)
        lane = lax.broadcasted_iota(jnp.int32, (1, LANES), 1)
        for pair in range(ATTN_HEADS // 2):
            lo = acc_sc[2 * pair] / l_sc[2 * pair]
            hi = acc_sc[2 * pair + 1] / l_sc[2 * pair + 1]
            out = jnp.where(lane < ATTN_HEAD_DIM, lo, hi)
            o_ref[0, :, pair * LANES:(pair + 1) * LANES] = out.astype(o_ref.dtype)


def _fox(q_aug, k_aug, v4):
    B, H, S, _ = q_aug.shape
    t = TQ_ATTN
    n = S // t
    kv_row = lambda b, i, j: jnp.minimum(j, i)
    return pl.pallas_call(
        _fox_kernel,
        grid=(B, n, n),
        in_specs=[pl.BlockSpec((1, H, t, LANES), lambda b, i, j: (b, 0, i, 0)),
                  pl.BlockSpec((1, H, t, LANES), lambda b, i, j: (b, 0, kv_row(b, i, j), 0)),
                  pl.BlockSpec((1, H // 2, t, LANES), lambda b, i, j: (b, 0, kv_row(b, i, j), 0))],
        out_specs=pl.BlockSpec((1, t, ATTN_WIDTH), lambda b, i, j: (b, i, 0)),
        out_shape=jax.ShapeDtypeStruct((B, S, ATTN_WIDTH), MXU_DTYPE),
        scratch_shapes=[pltpu.VMEM((ATTN_HEADS, t, LANES), F32),
                        pltpu.VMEM((ATTN_HEADS, t, LANES), F32),
                        pltpu.VMEM((ATTN_HEADS, t, LANES), F32)],
        compiler_params=_params("parallel", "parallel", "arbitrary"),
        name="fox_attention",
    )(q_aug, k_aug, v4)


POOL_HALO = 16


def _pool_kernel(u_ref, w_ref, scale_ref, y_ref, halo_sc):
    s_idx = pl.program_id(1)
    ts = u_ref.shape[1]

    @pl.when(s_idx == 0)
    def _():
        halo_sc[...] = jnp.zeros(halo_sc.shape, F32)

    u = u_ref[0]
    ext = jnp.concatenate([halo_sc[...], u], axis=0)
    halo_sc[...] = u[ts - POOL_HALO:, :]
    pos = (lax.broadcasted_iota(jnp.int32, (ts, 1), 0) + s_idx * ts + 1).astype(F32)
    for g, w in enumerate(POOL_WINDOWS):
        sl = slice(g * POOL_GROUP_DIM, (g + 1) * POOL_GROUP_DIM)
        acc = ext[:, sl]
        span = 1
        while span < w:
            acc = acc + pltpu.roll(acc, span, axis=0)
            span *= 2
        mean = acc[POOL_HALO:, :] / jnp.minimum(pos, float(w))
        d = mean - u[:, sl]
        y = _dot(_mx(d), w_ref[g]) * scale_ref[:, sl]
        y_ref[0, :, sl] = y.astype(y_ref.dtype)


def _pool(pu3, pool_w, pool_scale_row):
    B, S, _ = pu3.shape
    ts = TS_POOL
    return pl.pallas_call(
        _pool_kernel,
        grid=(B, S // ts),
        in_specs=[pl.BlockSpec((1, ts, POOL_WIDTH), lambda b, s: (b, s, 0)),
                  pl.BlockSpec(pool_w.shape, lambda b, s: (0, 0, 0)),
                  pl.BlockSpec((1, POOL_WIDTH), lambda b, s: (0, 0))],
        out_specs=pl.BlockSpec((1, ts, POOL_WIDTH), lambda b, s: (b, s, 0)),
        out_shape=jax.ShapeDtypeStruct((B, S, POOL_WIDTH), MXU_DTYPE),
        scratch_shapes=[pltpu.VMEM((POOL_HALO, POOL_WIDTH), F32)],
        compiler_params=_params("parallel", "arbitrary"),
        name="pool",
    )(pu3, pool_w, pool_scale_row)


DN_HALO = SUBLANES


def _unit_lower_inverses(lows):
    n = lows[0].shape[0]
    eye = (lax.broadcasted_iota(jnp.int32, (n, n), 0) == lax.broadcasted_iota(jnp.int32, (n, n), 1)).astype(F32)
    lows_b = [_mx(low) for low in lows]
    invs = [eye - low for low in lows]
    powers = [_dot(lb, lb) for lb in lows_b]
    span = 2
    while True:
        powers_b = [_mx(pw) for pw in powers]
        invs = [inv + _dot(_mx(inv), pb) for inv, pb in zip(invs, powers_b)]
        span *= 2
        if span >= n:
            return invs
        powers = [_dot(pb, pb) for pb in powers_b]


def _dn_kernel(qkv_ref, dg_ref, g_ref, gt_ref, cw_ref, nw_ref, o_ref, halo_sc, state_sc):
    c_idx = pl.program_id(0)
    NB, C = qkv_ref.shape[0], qkv_ref.shape[1]

    @pl.when(c_idx == 0)
    def _():
        halo_sc[...] = jnp.zeros(halo_sc.shape, F32)
        state_sc[...] = jnp.zeros(state_sc.shape, F32)

    ri = lax.broadcasted_iota(jnp.int32, (C, C), 0)
    ci = lax.broadcasted_iota(jnp.int32, (C, C), 1)
    causal = ri >= ci
    strict = ri > ci

    chains = [(b, h) for b in range(NB) for h in range(DN_HEADS)]
    qs, ks, vs = [], [], []
    for b in range(NB):
        u = qkv_ref[b]
        ext = jnp.concatenate([halo_sc[b], u], axis=0)
        halo_sc[b] = u[C - DN_HALO:, :]
        conv = u * cw_ref[DN_CONV - 1:DN_CONV, :]
        for d in range(1, DN_CONV):
            conv = conv + pltpu.roll(ext, d, axis=0)[DN_HALO:, :] * cw_ref[DN_CONV - 1 - d:DN_CONV - d, :]
        act = _silu(conv)
        for h in range(DN_HEADS):
            q = act[:, h * DN_HEAD_DIM:(h + 1) * DN_HEAD_DIM]
            k = act[:, DN_WIDTH + h * DN_HEAD_DIM:DN_WIDTH + (h + 1) * DN_HEAD_DIM]
            qs.append(q * lax.rsqrt(jnp.sum(q * q, axis=-1, keepdims=True) + NORM_EPS) * (DN_HEAD_DIM ** -0.5))
            ks.append(k * lax.rsqrt(jnp.sum(k * k, axis=-1, keepdims=True) + NORM_EPS))
            vs.append(act[:, 2 * DN_WIDTH + h * DN_HEAD_DIM:2 * DN_WIDTH + (h + 1) * DN_HEAD_DIM])

    gcs = [g_ref[b, :, DECAY_LANE + h:DECAY_LANE + h + 1] for b, h in chains]
    betas = [g_ref[b, :, BETA_LANE + h:BETA_LANE + h + 1] for b, h in chains]
    gc_rows = [gt_ref[b, DECAY_LANE + h:DECAY_LANE + h + 1, :] for b, h in chains]
    g_lasts = [gc[C - 1:C, :] for gc in gcs]
    decays = [jnp.where(causal, jnp.exp(jnp.where(causal, gc - gr, 0.0)), 0.0) for gc, gr in zip(gcs, gc_rows)]
    kbs = [_mx(k) for k in ks]
    kks = [_dot_nt(kb, kb) for kb in kbs]
    qks = [_dot_nt(_mx(q), kb) for q, kb in zip(qs, kbs)]
    lows = [jnp.where(strict, beta * kk * decay, 0.0) for beta, kk, decay in zip(betas, kks, decays)]
    attns = [_mx(jnp.where(causal, qk * decay, 0.0)) for qk, decay in zip(qks, decays)]
    invs = _unit_lower_inverses(lows)
    e_gcs = [jnp.exp(gc) for gc in gcs]
    rhss = [_mx(jnp.concatenate([v * beta, k * (beta * eg)], axis=1)) for v, k, beta, eg in zip(vs, ks, betas, e_gcs)]
    sols = [_dot(_mx(inv), rhs) for inv, rhs in zip(invs, rhss)]
    q_decs = [_mx(q * eg) for q, eg in zip(qs, e_gcs)]
    k_dec_ts = [_mx((k * jnp.exp(gl - gc)).T) for k, gl, gc in zip(ks, g_lasts, gcs)]

    states = [state_sc[b, h] for b, h in chains]
    states_b = [_mx(st) for st in states]
    u_news = [_mx(sol[:, :DN_HEAD_DIM] - _dot(_mx(sol[:, DN_HEAD_DIM:]), sb)) for sol, sb in zip(sols, states_b)]
    outs = [_dot(qd, sb) + _dot(at, un) for qd, sb, at, un in zip(q_decs, states_b, attns, u_news)]
    for (b, h), st, gl, kt, un in zip(chains, states, g_lasts, k_dec_ts, u_news):
        state_sc[b, h] = st * jnp.exp(gl) + _dot(kt, un)
    for (b, h), o in zip(chains, outs):
        sl = slice(h * DN_HEAD_DIM, (h + 1) * DN_HEAD_DIM)
        o = o * lax.rsqrt(jnp.mean(o * o, axis=-1, keepdims=True) + NORM_EPS) * nw_ref[...]
        o = o * _silu(dg_ref[b, :, sl])
        o_ref[b, :, sl] = o.astype(o_ref.dtype)


def _deltanet(dqkv3, dg3, gates, gates_t, conv_w, norm_w_row):
    B, S, _ = dqkv3.shape
    C = DN_CHUNK
    return pl.pallas_call(
        _dn_kernel,
        grid=(S // C,),
        in_specs=[pl.BlockSpec((B, C, 3 * DN_WIDTH), lambda c: (0, c, 0)),
                  pl.BlockSpec((B, C, DN_WIDTH), lambda c: (0, c, 0)),
                  pl.BlockSpec((B, C, LANES), lambda c: (0, c, 0)),
                  pl.BlockSpec((B, GATE_ROWS, C), lambda c: (0, 0, c)),
                  pl.BlockSpec((DN_CONV, 3 * DN_WIDTH), lambda c: (0, 0)),
                  pl.BlockSpec((1, DN_HEAD_DIM), lambda c: (0, 0))],
        out_specs=pl.BlockSpec((B, C, DN_WIDTH), lambda c: (0, c, 0)),
        out_shape=jax.ShapeDtypeStruct((B, S, DN_WIDTH), MXU_DTYPE),
        scratch_shapes=[pltpu.VMEM((B, DN_HALO, 3 * DN_WIDTH), F32),
                        pltpu.VMEM((B, DN_HEADS, DN_HEAD_DIM, DN_HEAD_DIM), F32)],
        compiler_params=_params("arbitrary"),
        name="deltanet",
    )(dqkv3, dg3, gates, gates_t, conv_w, norm_w_row)


ROUTE_E1, ROUTE_E2, ROUTE_G1, ROUTE_G2 = range(4)
ROUTER_EXPERT_LANE = N_GROUPS


def _merge_kernel(x_ref, ya_ref, yp_ref, yd_ref, p_ref, wgate_ref, wa_ref, wp_ref, wd_ref, wo_ref,
                  g1_ref, b1_ref, wr_ref, br_ref, wpg_ref, wpp_ref,
                  x1_ref, base_ref, route_ref):
    tm = x_ref.shape[0]
    x = x_ref[...]
    gate = _dot(_mx(x), wgate_ref[...])
    merged = (_sigmoid(gate[:, :D_MODEL]) * _dot(ya_ref[...], wa_ref[...])
              + _sigmoid(gate[:, D_MODEL:2 * D_MODEL]) * _dot(yp_ref[...], wp_ref[...])
              + _sigmoid(gate[:, 2 * D_MODEL:]) * _dot(yd_ref[...], wd_ref[...]))
    mix = _dot(_mx(merged), wo_ref[...])
    x1 = _layer_norm(DEEPNORM_ALPHA * x + mix, g1_ref[...], b1_ref[...])
    x1_ref[...] = x1
    x1b = _mx(x1)
    ple = _sigmoid(_dot(x1b, wpg_ref[...])) * _dot(_mx(p_ref[...]), wpp_ref[...])
    base_ref[...] = DEEPNORM_ALPHA * x1 + ple

    logits = _dot(x1b, wr_ref[...]) + br_ref[...]
    lane = lax.broadcasted_iota(jnp.int32, (tm, LANES), 1)
    ninf = -jnp.inf

    def first_argmax(vals):
        top = jnp.max(vals, axis=-1, keepdims=True)
        idx = jnp.min(jnp.where(vals == top, lane, LANES), axis=-1, keepdims=True)
        return top, idx

    group_logits = jnp.where(lane < N_GROUPS, logits, ninf)
    g_top, g_idx = first_argmax(group_logits)
    p_group = 1.0 / jnp.sum(jnp.exp(group_logits - g_top), axis=-1, keepdims=True)
    lo = ROUTER_EXPERT_LANE + EXPERTS_PER_GROUP * g_idx
    local = jnp.where((lane >= lo) & (lane < lo + EXPERTS_PER_GROUP), logits, ninf)
    top1, idx1 = first_argmax(local)
    top2, idx2 = first_argmax(jnp.where(lane == idx1, ninf, local))
    z = jnp.sum(jnp.exp(local - top1), axis=-1, keepdims=True)
    prob1 = 1.0 / z
    prob2 = jnp.exp(top2 - top1) / z
    gate1 = p_group * prob1 / (prob1 + prob2)
    gate2 = p_group * prob2 / (prob1 + prob2)
    route = jnp.zeros((tm, LANES), F32)
    for pos, val in ((ROUTE_E1, (idx1 - ROUTER_EXPERT_LANE).astype(F32)),
                     (ROUTE_E2, (idx2 - ROUTER_EXPERT_LANE).astype(F32)),
                     (ROUTE_G1, gate1), (ROUTE_G2, gate2)):
        route = jnp.where(lane == pos, val, route)
    route_ref[...] = route


def _merge(x2, ya, yp, yd, p2, wgate, wa, wp, wd, wo, g1, b1, wr, br, wpg, wpp):
    T = x2.shape[0]
    tm = TM_MERGE
    row = lambda n: pl.BlockSpec((tm, n), lambda i: (i, 0))
    consts = [wgate, wa, wp, wd, wo, g1, b1, wr, br, wpg, wpp]
    return pl.pallas_call(
        _merge_kernel,
        grid=(T // tm,),
        in_specs=[row(D_MODEL), row(ATTN_WIDTH), row(POOL_WIDTH), row(DN_WIDTH), row(PLE_DIM)]
                 + [_const_spec(c.shape) for c in consts],
        out_specs=[row(D_MODEL), row(D_MODEL), row(LANES)],
        out_shape=[jax.ShapeDtypeStruct((T, D_MODEL), F32),
                   jax.ShapeDtypeStruct((T, D_MODEL), F32),
                   jax.ShapeDtypeStruct((T, LANES), F32)],
        compiler_params=_params("parallel"),
        name="merge_router",
    )(x2, ya, yp, yd, p2, *consts)


def _dest_kernel(route_ref, d1_ref, d2_ref, pend_ref, cnt_sc, start_sc):
    phase = pl.program_id(0)
    step = pl.program_id(1)
    tm = route_ref.shape[0]
    lane = lax.broadcasted_iota(jnp.int32, (tm, LANES), 1)
    e1 = route_ref[:, ROUTE_E1:ROUTE_E1 + 1].astype(jnp.int32)
    e2 = route_ref[:, ROUTE_E2:ROUTE_E2 + 1].astype(jnp.int32)
    onehot = ((lane == e1) | (lane == e2)).astype(F32)

    @pl.when((phase == 0) & (step == 0))
    def _():
        cnt_sc[...] = jnp.zeros(cnt_sc.shape, F32)

    @pl.when(phase == 0)
    def _():
        cnt_sc[...] = cnt_sc[...] + jnp.sum(onehot, axis=0, keepdims=True)
        d1_ref[...] = jnp.zeros(d1_ref.shape, jnp.int32)
        d2_ref[...] = jnp.zeros(d2_ref.shape, jnp.int32)
        pend_ref[...] = jnp.zeros(pend_ref.shape, F32)

    @pl.when((phase == 1) & (step == 0))
    def _():
        blk = float(MOE_BLOCK)
        padded = jnp.floor((cnt_sc[...] + (blk - 1.0)) / blk) * blk
        ri = lax.broadcasted_iota(jnp.int32, (LANES, LANES), 0)
        ci = lax.broadcasted_iota(jnp.int32, (LANES, LANES), 1)
        ends = _dot_exact(jnp.broadcast_to(padded, (SUBLANES, LANES)), (ri <= ci).astype(F32))
        start_sc[...] = ends[:1] - padded
        cnt_sc[...] = ends[:1]

    @pl.when(phase == 1)
    def _():
        ri = lax.broadcasted_iota(jnp.int32, (tm, tm), 0)
        ci = lax.broadcasted_iota(jnp.int32, (tm, tm), 1)
        before = _dot((ri > ci).astype(jnp.bfloat16), onehot.astype(jnp.bfloat16)) + start_sc[...]
        dest = jnp.where(lane == 0, jnp.sum(jnp.where(lane == e1, before, 0.0), axis=-1, keepdims=True),
                         jnp.where(lane == 1, jnp.sum(jnp.where(lane == e2, before, 0.0), axis=-1, keepdims=True), 0.0))
        pick = (lax.broadcasted_iota(jnp.int32, (SUBLANES, LANES), 1)
                == lax.broadcasted_iota(jnp.int32, (SUBLANES, LANES), 0)).astype(F32)
        rows = lax.dot_general(pick, dest, (((1,), (1,)), ((), ())), preferred_element_type=F32,
                               precision=lax.Precision.HIGHEST)
        d1_ref[...] = rows[0:1, :].astype(jnp.int32)
        d2_ref[...] = rows[1:2, :].astype(jnp.int32)
        start_sc[...] = start_sc[...] + jnp.sum(onehot, axis=0, keepdims=True)
        pend_ref[...] = jnp.broadcast_to(cnt_sc[...], pend_ref.shape)


def _dest(route):
    T = route.shape[0]
    tm = TM_MERGE
    return pl.pallas_call(
        _dest_kernel,
        grid=(2, T // tm),
        in_specs=[pl.BlockSpec((tm, LANES), lambda ph, i: (i, 0))],
        out_specs=[pl.BlockSpec((1, tm), lambda ph, i: (0, i * ph)),
                   pl.BlockSpec((1, tm), lambda ph, i: (0, i * ph)),
                   pl.BlockSpec((SUBLANES, LANES), lambda ph, i: (0, 0))],
        out_shape=[jax.ShapeDtypeStruct((1, T), jnp.int32),
                   jax.ShapeDtypeStruct((1, T), jnp.int32),
                   jax.ShapeDtypeStruct((SUBLANES, LANES), F32)],
        scratch_shapes=[pltpu.VMEM((1, LANES), F32), pltpu.VMEM((1, LANES), F32)],
        compiler_params=_params("arbitrary", "arbitrary"),
        name="moe_dest",
    )(route)


ISSUE_UNROLL = 8


def _row_copy(src, src_row, dst, dst_row, sem):
    return pltpu.make_async_copy(src.at[pl.ds(src_row, 1), :], dst.at[pl.ds(dst_row, 1), :], sem)


def _dispatch_kernel(d1_ref, d2_ref, x_ref, xs_in_ref, xs_ref, sem):
    del xs_in_ref
    tm = x_ref.shape[0]

    def issue(t, carry):
        _row_copy(x_ref, t, xs_ref, d1_ref[0, 0, t], sem.at[0]).start()
        _row_copy(x_ref, t, xs_ref, d2_ref[0, 0, t], sem.at[1]).start()
        return carry

    lax.fori_loop(0, tm, issue, 0, unroll=ISSUE_UNROLL)
    pltpu.make_async_copy(x_ref, xs_ref.at[pl.ds(0, tm), :], sem.at[0]).wait()
    pltpu.make_async_copy(x_ref, xs_ref.at[pl.ds(0, tm), :], sem.at[1]).wait()


def _dispatch(x1, dest1, dest2, n_rows):
    T = x1.shape[0]
    tm = TM_ROWS
    idx_spec = pl.BlockSpec((1, 1, tm), lambda i: (i, 0, 0), memory_space=pltpu.SMEM)
    xs0 = jnp.zeros((n_rows, D_MODEL), F32)
    return pl.pallas_call(
        _dispatch_kernel,
        grid=(T // tm,),
        in_specs=[idx_spec, idx_spec, pl.BlockSpec((tm, D_MODEL), lambda i: (i, 0)),
                  pl.BlockSpec(memory_space=pl.ANY)],
        out_specs=pl.BlockSpec(memory_space=pl.ANY),
        out_shape=jax.ShapeDtypeStruct((n_rows, D_MODEL), F32),
        scratch_shapes=[pltpu.SemaphoreType.DMA((2,))],
        input_output_aliases={3: 0},
        compiler_params=_params("arbitrary"),
        name="moe_dispatch",
    )(dest1.reshape(T // tm, 1, tm), dest2.reshape(T // tm, 1, tm), x1, xs0)


def _experts_kernel(bexp_ref, nreal_ref, xs_ref, wg_ref, wu_ref, wd_ref, ys_ref, wg_sc, wu_sc, wd_sc):
    i = pl.program_id(0)
    changed = (i == 0) | (bexp_ref[i] != bexp_ref[jnp.maximum(i - 1, 0)])

    @pl.when(changed)
    def _():
        wg_sc[...] = _mx(wg_ref[0, 0])
        wu_sc[...] = _mx(wu_ref[0, 0])
        wd_sc[...] = _mx(wd_ref[0, 0])

    @pl.when(i < nreal_ref[0])
    def _():
        xb = _mx(xs_ref[...])
        hid = _silu(_dot(xb, wg_sc[...])) * _dot(xb, wu_sc[...])
        ys_ref[...] = _dot(_mx(hid), wd_sc[...])

    @pl.when(i >= nreal_ref[0])
    def _():
        ys_ref[...] = jnp.zeros(ys_ref.shape, F32)


def _experts(xs, block_expert, n_real, layer, w_gate, w_up, w_down):
    n_rows = xs.shape[0]
    blk = MOE_BLOCK
    grid_spec = pltpu.PrefetchScalarGridSpec(
        num_scalar_prefetch=2,
        grid=(n_rows // blk,),
        in_specs=[pl.BlockSpec((blk, D_MODEL), lambda i, be, nr: (i, 0)),
                  pl.BlockSpec((1, 1, D_MODEL, D_EXPERT), lambda i, be, nr: (layer, be[i], 0, 0)),
                  pl.BlockSpec((1, 1, D_MODEL, D_EXPERT), lambda i, be, nr: (layer, be[i], 0, 0)),
                  pl.BlockSpec((1, 1, D_EXPERT, D_MODEL), lambda i, be, nr: (layer, be[i], 0, 0))],
        out_specs=pl.BlockSpec((blk, D_MODEL), lambda i, be, nr: (i, 0)),
        scratch_shapes=[pltpu.VMEM((D_MODEL, D_EXPERT), MXU_DTYPE),
                        pltpu.VMEM((D_MODEL, D_EXPERT), MXU_DTYPE),
                        pltpu.VMEM((D_EXPERT, D_MODEL), MXU_DTYPE)],
    )
    return pl.pallas_call(
        _experts_kernel,
        grid_spec=grid_spec,
        out_shape=jax.ShapeDtypeStruct((n_rows, D_MODEL), F32),
        compiler_params=_params("arbitrary"),
        name="moe_experts",
    )(block_expert, n_real, xs, w_gate, w_up, w_down)


def _combine_kernel(d1_ref, d2_ref, route_ref, base_ref, ys_ref, g2_ref, b2_ref, o_ref, buf, sem):
    tm = base_ref.shape[0]

    def issue(t, carry):
        _row_copy(ys_ref, d1_ref[0, 0, t], buf.at[0], t, sem.at[0]).start()
        _row_copy(ys_ref, d2_ref[0, 0, t], buf.at[1], t, sem.at[1]).start()
        return carry

    lax.fori_loop(0, tm, issue, 0, unroll=ISSUE_UNROLL)
    pltpu.make_async_copy(ys_ref.at[pl.ds(0, tm), :], buf.at[0], sem.at[0]).wait()
    pltpu.make_async_copy(ys_ref.at[pl.ds(0, tm), :], buf.at[1], sem.at[1]).wait()

    gate1 = route_ref[:, ROUTE_G1:ROUTE_G1 + 1]
    gate2 = route_ref[:, ROUTE_G2:ROUTE_G2 + 1]
    moe = gate1 * buf[0] + gate2 * buf[1]
    o_ref[...] = _layer_norm(base_ref[...] + moe, g2_ref[...], b2_ref[...])


def _combine(dest1, dest2, route, base, ys, g2, b2):
    T = base.shape[0]
    tm = TM_ROWS
    idx_spec = pl.BlockSpec((1, 1, tm), lambda i: (i, 0, 0), memory_space=pltpu.SMEM)
    return pl.pallas_call(
        _combine_kernel,
        grid=(T // tm,),
        in_specs=[idx_spec, idx_spec,
                  pl.BlockSpec((tm, LANES), lambda i: (i, 0)),
                  pl.BlockSpec((tm, D_MODEL), lambda i: (i, 0)),
                  pl.BlockSpec(memory_space=pl.ANY),
                  pl.BlockSpec((1, D_MODEL), lambda i: (0, 0)),
                  pl.BlockSpec((1, D_MODEL), lambda i: (0, 0))],
        out_specs=pl.BlockSpec((tm, D_MODEL), lambda i: (i, 0)),
        out_shape=jax.ShapeDtypeStruct((T, D_MODEL), F32),
        scratch_shapes=[pltpu.VMEM((2, tm, D_MODEL), F32), pltpu.SemaphoreType.DMA((2,))],
        compiler_params=_params("arbitrary"),
        name="moe_combine",
    )(dest1.reshape(T // tm, 1, tm), dest2.reshape(T // tm, 1, tm), route, base, ys, g2, b2)


def _pad_lanes(a):
    return jnp.pad(a, ((0, 0), (0, LANES - a.shape[1])))


def _layer(layer, x2, p2, B, S, w_in, b_forget, pool_w, pool_scale, dn_conv, dn_a_log, dn_dt_bias, dn_norm_w,
           w_br_attn, w_br_pool, w_br_dn, w_out, ln1_g, ln1_b, w_rg, b_rg, w_re, b_re,
           w_exp_gate, w_exp_up, w_exp_down, w_pp, w_pg, ln2_g, ln2_b):
    T = B * S
    o = 0
    cols = {}
    for name, width in (("attn", 3 * ATTN_WIDTH), ("forget", ATTN_HEADS), ("pool", POOL_WIDTH),
                        ("dn", 3 * DN_WIDTH), ("decay", DN_HEADS), ("beta", DN_HEADS),
                        ("dgate", DN_WIDTH), ("merge", 3 * D_MODEL)):
        cols[name] = w_in[:, o:o + width]
        o += width
    w_small = _pad_lanes(jnp.concatenate([cols["forget"], cols["decay"], cols["beta"]], axis=1))
    qkv, pu, dqkv, dg, small = _inproj(x2, _mx(cols["attn"]), _mx(cols["pool"]), _mx(cols["dn"]),
                                       _mx(cols["dgate"]), _mx(w_small))

    bias_row = _pad_lanes(jnp.concatenate([b_forget, dn_dt_bias])[None, :])
    alog_row = _pad_lanes(jnp.concatenate([jnp.zeros((ATTN_HEADS,), F32), dn_a_log])[None, :])
    gates, gates_t, q_aug, k_aug, v4 = _gates(small.reshape(B, S, LANES), bias_row, alog_row,
                                              qkv.reshape(B, S, 3 * ATTN_WIDTH))

    y_attn = _fox(q_aug, k_aug, v4)
    y_pool = _pool(pu.reshape(B, S, POOL_WIDTH), _mx(pool_w), pool_scale[None, :])
    y_dn = _deltanet(dqkv.reshape(B, S, 3 * DN_WIDTH), dg.reshape(B, S, DN_WIDTH), gates, gates_t,
                     dn_conv, dn_norm_w[None, :])

    w_router = _pad_lanes(jnp.concatenate([w_rg, w_re], axis=1))
    b_router = _pad_lanes(jnp.concatenate([b_rg, b_re])[None, :])
    x1, base, route = _merge(
        x2, y_attn.reshape(T, ATTN_WIDTH), y_pool.reshape(T, POOL_WIDTH), y_dn.reshape(T, DN_WIDTH), p2,
        _mx(cols["merge"]), _mx(w_br_attn), _mx(w_br_pool), _mx(w_br_dn), _mx(w_out),
        ln1_g[None, :], ln1_b[None, :], _mx(w_router), b_router, _mx(w_pg), _mx(w_pp))

    dest1, dest2, pend = _dest(route)
    blk = MOE_BLOCK
    n_rows = 2 * T + N_EXPERTS * blk
    padded_end = pend[0, :N_EXPERTS].astype(jnp.int32)
    block_start = jnp.arange(n_rows // blk, dtype=jnp.int32) * blk
    block_expert = jnp.minimum(jnp.sum(block_start[:, None] >= padded_end[None, :], axis=1),
                               N_EXPERTS - 1).astype(jnp.int32)
    n_real = (padded_end[-1:] // blk).astype(jnp.int32)

    xs = _dispatch(x1, dest1, dest2, n_rows)
    ys = _experts(xs, block_expert, n_real, layer, w_exp_gate, w_exp_up, w_exp_down)
    return _combine(dest1, dest2, route, base, ys, ln2_g[None, :], ln2_b[None, :])


def kernel(x, p, w_in, b_forget, pool_w, pool_scale, dn_conv, dn_a_log, dn_dt_bias, dn_norm_w, w_br_attn, w_br_pool, w_br_dn, w_out, ln1_g, ln1_b, w_router_group, b_router_group, w_router_expert, b_router_expert, w_exp_gate, w_exp_up, w_exp_down, w_ple_proj, w_ple_gate, ln2_g, ln2_b):
    B, S, _ = x.shape
    x2 = x.reshape(B * S, D_MODEL)
    for i in range(DEPTH):
        x2 = _layer(i, x2, p[i].reshape(B * S, PLE_DIM), B, S, w_in[i], b_forget[i], pool_w[i], pool_scale[i],
                    dn_conv[i], dn_a_log[i], dn_dt_bias[i], dn_norm_w[i], w_br_attn[i], w_br_pool[i],
                    w_br_dn[i], w_out[i], ln1_g[i], ln1_b[i], w_router_group[i], b_router_group[i],
                    w_router_expert[i], b_router_expert[i], w_exp_gate, w_exp_up, w_exp_down,
                    w_ple_proj[i], w_ple_gate[i], ln2_g[i], ln2_b[i])
    return x2.reshape(B, S, D_MODEL)
```

```python
import functools

import numpy as np
import jax
import jax.numpy as jnp
from jax import lax
from jax.experimental import pallas as pl
from jax.experimental.pallas import tpu as pltpu

F32 = jnp.float32
MXU_DTYPE = jnp.bfloat16

D_MODEL = 1024
DEPTH = 4
PLE_DIM = 256
ATTN_HEADS = 8
ATTN_HEAD_DIM = 64
ATTN_WIDTH = ATTN_HEADS * ATTN_HEAD_DIM
POOL_WINDOWS = (2, 4, 8, 16)
POOL_GROUP_DIM = 128
POOL_WIDTH = len(POOL_WINDOWS) * POOL_GROUP_DIM
DN_HEADS = 4
DN_HEAD_DIM = 128
DN_WIDTH = DN_HEADS * DN_HEAD_DIM
DN_CONV = 4
N_GROUPS = 4
EXPERTS_PER_GROUP = 8
N_EXPERTS = N_GROUPS * EXPERTS_PER_GROUP
D_EXPERT = 512
DEEPNORM_ALPHA = (2 * DEPTH) ** 0.25
LN_EPS = 1e-5
NORM_EPS = 1e-6
NEG_INF = -1e30
LOG2_E = 1.4426950408889634

LANES = 128
SUBLANES = 8
VMEM_LIMIT = 56 * 1024 * 1024

FORGET_LANE = 0
DECAY_LANE = ATTN_HEADS
BETA_LANE = ATTN_HEADS + DN_HEADS
GATE_ROWS = 16

TM_PROJ = 256
TQ_ATTN = 512
TS_POOL = 512
DN_CHUNK = 128
TM_MERGE = 256
TM_ROWS = 1024
TM_DEST = 512
MOE_BLOCK = 256
CUM_ROWS = 256


def _dot(a, b):
    return jnp.dot(a, b, preferred_element_type=F32)


def _dot_nt(a, b):
    return lax.dot_general(a, b, (((1,), (1,)), ((), ())), preferred_element_type=F32)


def _dot_exact(a, b):
    return jnp.dot(a, b, preferred_element_type=F32, precision=lax.Precision.HIGHEST)


def _mx(a):
    return a.astype(MXU_DTYPE)


def _sigmoid(z):
    return 1.0 / (1.0 + jnp.exp(-z))


def _silu(z):
    return z * _sigmoid(z)


def _layer_norm(h, g, b):
    mu = jnp.mean(h, axis=-1, keepdims=True)
    hc = h - mu
    var = jnp.mean(hc * hc, axis=-1, keepdims=True)
    return hc * lax.rsqrt(var + LN_EPS) * g + b


def _const_spec(shape):
    zeros = (0,) * len(shape)
    return pl.BlockSpec(shape, lambda *_: zeros, pipeline_mode=pl.Buffered(1))


def _params(*sem):
    return pltpu.CompilerParams(dimension_semantics=sem, vmem_limit_bytes=VMEM_LIMIT)


def _inproj_kernel(x_ref, wa_ref, wp_ref, wd_ref, wg_ref, ws_ref,
                   qkv_ref, pu_ref, dqkv_ref, dg_ref, small_ref):
    xb = _mx(x_ref[...])
    qkv_ref[...] = _dot(xb, wa_ref[...]).astype(qkv_ref.dtype)
    pu_ref[...] = _dot(xb, wp_ref[...])
    dqkv_ref[...] = _dot(xb, wd_ref[...])
    dg_ref[...] = _dot(xb, wg_ref[...])
    small_ref[...] = _dot(xb, ws_ref[...])


def _inproj(x2, wa, wp, wd, wg, ws):
    T = x2.shape[0]
    tm = TM_PROJ
    row = lambda n: pl.BlockSpec((tm, n), lambda i: (i, 0))
    return pl.pallas_call(
        _inproj_kernel,
        grid=(T // tm,),
        in_specs=[row(D_MODEL), _const_spec(wa.shape), _const_spec(wp.shape), _const_spec(wd.shape),
                  _const_spec(wg.shape), _const_spec(ws.shape)],
        out_specs=[row(3 * ATTN_WIDTH), row(POOL_WIDTH), row(3 * DN_WIDTH), row(DN_WIDTH), row(LANES)],
        out_shape=[jax.ShapeDtypeStruct((T, 3 * ATTN_WIDTH), MXU_DTYPE),
                   jax.ShapeDtypeStruct((T, POOL_WIDTH), F32),
                   jax.ShapeDtypeStruct((T, 3 * DN_WIDTH), F32),
                   jax.ShapeDtypeStruct((T, DN_WIDTH), F32),
                   jax.ShapeDtypeStruct((T, LANES), F32)],
        compiler_params=_params("parallel"),
        name="inproj",
    )(x2, wa, wp, wd, wg, ws)


BIAS_SPLIT_LANES = 3 * ATTN_HEADS
BIAS_ONES_LANE = BIAS_SPLIT_LANES


def _sum_lane(half):
    return (1 - half) * ATTN_HEAD_DIM


def _bias_placement():
    pq = np.zeros((LANES, ATTN_HEADS * LANES), np.float32)
    pk = np.zeros((LANES, ATTN_HEADS * LANES), np.float32)
    for h in range(ATTN_HEADS):
        base = h * LANES + (1 - h % 2) * ATTN_HEAD_DIM
        for part in range(3):
            pq[part * ATTN_HEADS + h, base + part] = 1.0
            pq[BIAS_ONES_LANE, base + 3 + part] = 1.0
            pk[BIAS_ONES_LANE, base + part] = 1.0
            pk[part * ATTN_HEADS + h, base + 3 + part] = -1.0
    return pq, pk


def _gates_kernel(small_ref, bias_ref, alog_ref, qkv_ref, pq_ref, pk_ref,
                  g_ref, gt_ref, qa_ref, ka_ref, va_ref, carry_sc):
    R = small_ref.shape[1]

    @pl.when(pl.program_id(1) == 0)
    def _():
        carry_sc[...] = jnp.zeros(carry_sc.shape, F32)

    ri = lax.broadcasted_iota(jnp.int32, (R, R), 0)
    ci = lax.broadcasted_iota(jnp.int32, (R, R), 1)
    tri_full = (ri >= ci).astype(F32)
    tri_chunk = ((ri >= ci) & (ri // DN_CHUNK == ci // DN_CHUNK)).astype(F32)
    lane = lax.broadcasted_iota(jnp.int32, (R, LANES), 1)
    is_forget = lane < DECAY_LANE
    is_decay = (lane >= DECAY_LANE) & (lane < BETA_LANE)

    z = small_ref[0] + bias_ref[...]
    soft = jnp.log(1.0 + jnp.exp(-jnp.abs(z)))
    log_f = jnp.minimum(z, 0.0) - soft
    decay = -jnp.exp(alog_ref[...]) * (jnp.maximum(z, 0.0) + soft)
    vals = jnp.where(is_forget, log_f, jnp.where(is_decay, decay, _sigmoid(z)))
    cum_full = _dot_exact(tri_full, vals) + carry_sc[...]
    cum_chunk = _dot_exact(tri_chunk, vals)
    carry_sc[...] = cum_full[R - 1:R, :]
    out = jnp.where(is_forget, cum_full, jnp.where(is_decay, cum_chunk, vals))
    g_ref[0] = out
    gt_ref[0] = out.T[:GATE_ROWS, :]

    c = jnp.where(is_forget, out, 0.0) * LOG2_E
    hi = _mx(c).astype(F32)
    mid = _mx(c - hi).astype(F32)
    lo = _mx(c - hi - mid).astype(F32)
    split = (hi + pltpu.roll(mid, ATTN_HEADS, axis=1) + pltpu.roll(lo, 2 * ATTN_HEADS, axis=1)
             + jnp.where(lane == BIAS_ONES_LANE, 1.0, 0.0))
    split_b = _mx(split)
    extra_q = _dot(split_b, pq_ref[...]).astype(qa_ref.dtype)
    extra_k = _dot(split_b, pk_ref[...]).astype(ka_ref.dtype)
    lane1 = lax.broadcasted_iota(jnp.int32, (1, LANES), 1)
    for pair in range(ATTN_HEADS // 2):
        qp = qkv_ref[0, :, pair * LANES:(pair + 1) * LANES]
        kp = qkv_ref[0, :, ATTN_WIDTH + pair * LANES:ATTN_WIDTH + (pair + 1) * LANES]
        vp = qkv_ref[0, :, 2 * ATTN_WIDTH + pair * LANES:2 * ATTN_WIDTH + (pair + 1) * LANES]
        for half in range(2):
            h = 2 * pair + half
            own = (lane1 >= half * ATTN_HEAD_DIM) & (lane1 < (half + 1) * ATTN_HEAD_DIM)
            ones = jnp.where(lane1 == _sum_lane(half), 1.0, 0.0).astype(va_ref.dtype)
            qa_ref[0, h] = jnp.where(own, qp, extra_q[:, h * LANES:(h + 1) * LANES])
            ka_ref[0, h] = jnp.where(own, kp, extra_k[:, h * LANES:(h + 1) * LANES])
            va_ref[0, h] = jnp.where(own, vp, ones)


def _gates(small3, bias_row, alog_row, qkv3):
    B, S, _ = small3.shape
    R = CUM_ROWS
    pq, pk = _bias_placement()
    pq = jnp.asarray(pq, MXU_DTYPE)
    pk = jnp.asarray(pk, MXU_DTYPE)
    H = ATTN_HEADS
    return pl.pallas_call(
        _gates_kernel,
        grid=(B, S // R),
        in_specs=[pl.BlockSpec((1, R, LANES), lambda b, r: (b, r, 0)),
                  pl.BlockSpec((1, LANES), lambda b, r: (0, 0)),
                  pl.BlockSpec((1, LANES), lambda b, r: (0, 0)),
                  pl.BlockSpec((1, R, 3 * ATTN_WIDTH), lambda b, r: (b, r, 0)),
                  pl.BlockSpec(pq.shape, lambda b, r: (0, 0)),
                  pl.BlockSpec(pk.shape, lambda b, r: (0, 0))],
        out_specs=[pl.BlockSpec((1, R, LANES), lambda b, r: (b, r, 0)),
                   pl.BlockSpec((1, GATE_ROWS, R), lambda b, r: (b, 0, r)),
                   pl.BlockSpec((1, H, R, LANES), lambda b, r: (b, 0, r, 0)),
                   pl.BlockSpec((1, H, R, LANES), lambda b, r: (b, 0, r, 0)),
                   pl.BlockSpec((1, H, R, LANES), lambda b, r: (b, 0, r, 0))],
        out_shape=[jax.ShapeDtypeStruct((B, S, LANES), F32),
                   jax.ShapeDtypeStruct((B, GATE_ROWS, S), F32),
                   jax.ShapeDtypeStruct((B, H, S, LANES), MXU_DTYPE),
                   jax.ShapeDtypeStruct((B, H, S, LANES), MXU_DTYPE),
                   jax.ShapeDtypeStruct((B, H, S, LANES), MXU_DTYPE)],
        scratch_shapes=[pltpu.VMEM((1, LANES), F32)],
        compiler_params=_params("parallel", "arbitrary"),
        name="gates",
    )(small3, bias_row, alog_row, qkv3, pq, pk)


def _fox_kernel(qi_ref, kj_ref, q_ref, k_ref, v_ref, o_ref, m_sc, acc_sc):
    step = pl.program_id(1)
    i = qi_ref[step]
    j = kj_ref[step]
    tq = q_ref.shape[2]
    tk = k_ref.shape[2]
    reps = tk // LANES

    @pl.when(j == 0)
    def _():
        m_sc[...] = jnp.full(m_sc.shape, NEG_INF, F32)
        acc_sc[...] = jnp.zeros(acc_sc.shape, F32)

    def sweep(masked):
        if masked:
            rows = lax.broadcasted_iota(jnp.int32, (tq, tk), 0)
            cols = lax.broadcasted_iota(jnp.int32, (tq, tk), 1)
            keep = cols <= rows
        for h in range(ATTN_HEADS):
            s = _dot_nt(q_ref[0, h], k_ref[0, h])
            if masked:
                s = jnp.where(keep, s, NEG_INF)
            m_prev = m_sc[h]
            m_next = jnp.maximum(m_prev, jnp.max(s, axis=1)[:, None])
            p = jnp.exp2(s - jnp.concatenate([m_next] * reps, axis=1))
            m_sc[h] = m_next
            acc_sc[h] = jnp.exp2(m_prev - m_next) * acc_sc[h] + _dot(_mx(p), v_ref[0, h])

    @pl.when(j < i)
    def _():
        sweep(False)

    @pl.when(j == i)
    def _():
        sweep(True)
        lane = lax.broadcasted_iota(jnp.int32, (1, LANES), 1)
        for pair in range(ATTN_HEADS // 2):
            halves = []
            for half in range(2):
                acc = acc_sc[2 * pair + half]
                halves.append(acc / acc[:, _sum_lane(half):_sum_lane(half) + 1])
            out = jnp.where(lane < ATTN_HEAD_DIM, halves[0], halves[1])
            o_ref[0, :, pair * LANES:(pair + 1) * LANES] = out.astype(o_ref.dtype)


def _fox(q_aug, k_aug, v_aug):
    B, H, S, _ = q_aug.shape
    t = TQ_ATTN
    n = S // t
    pairs = [(i, j) for i in range(n) for j in range(i + 1)]
    qi = jnp.asarray([i for i, _ in pairs], jnp.int32)
    kj = jnp.asarray([j for _, j in pairs], jnp.int32)
    grid_spec = pltpu.PrefetchScalarGridSpec(
        num_scalar_prefetch=2,
        grid=(B, len(pairs)),
        in_specs=[pl.BlockSpec((1, H, t, LANES), lambda b, s, qi, kj: (b, 0, qi[s], 0)),
                  pl.BlockSpec((1, H, t, LANES), lambda b, s, qi, kj: (b, 0, kj[s], 0)),
                  pl.BlockSpec((1, H, t, LANES), lambda b, s, qi, kj: (b, 0, kj[s], 0))],
        out_specs=pl.BlockSpec((1, t, ATTN_WIDTH), lambda b, s, qi, kj: (b, qi[s], 0)),
        scratch_shapes=[pltpu.VMEM((ATTN_HEADS, t, LANES), F32),
                        pltpu.VMEM((ATTN_HEADS, t, LANES), F32)],
    )
    return pl.pallas_call(
        _fox_kernel,
        grid_spec=grid_spec,
        out_shape=jax.ShapeDtypeStruct((B, S, ATTN_WIDTH), MXU_DTYPE),
        compiler_params=_params("parallel", "arbitrary"),
        name="fox_attention",
    )(qi, kj, q_aug, k_aug, v_aug)


POOL_HALO = 16


def _pool_kernel(u_ref, w_ref, scale_ref, y_ref, halo_sc):
    s_idx = pl.program_id(1)
    ts = u_ref.shape[1]

    @pl.when(s_idx == 0)
    def _():
        halo_sc[...] = jnp.zeros(halo_sc.shape, F32)

    u = u_ref[0]
    ext = jnp.concatenate([halo_sc[...], u], axis=0)
    halo_sc[...] = u[ts - POOL_HALO:, :]
    pos = (lax.broadcasted_iota(jnp.int32, (ts, 1), 0) + s_idx * ts + 1).astype(F32)
    for g, w in enumerate(POOL_WINDOWS):
        sl = slice(g * POOL_GROUP_DIM, (g + 1) * POOL_GROUP_DIM)
        acc = ext[:, sl]
        span = 1
        while span < w:
            acc = acc + pltpu.roll(acc, span, axis=0)
            span *= 2
        mean = acc[POOL_HALO:, :] / jnp.minimum(pos, float(w))
        d = mean - u[:, sl]
        y = _dot(_mx(d), w_ref[g]) * scale_ref[:, sl]
        y_ref[0, :, sl] = y.astype(y_ref.dtype)


def _pool(pu3, pool_w, pool_scale_row):
    B, S, _ = pu3.shape
    ts = TS_POOL
    return pl.pallas_call(
        _pool_kernel,
        grid=(B, S // ts),
        in_specs=[pl.BlockSpec((1, ts, POOL_WIDTH), lambda b, s: (b, s, 0)),
                  pl.BlockSpec(pool_w.shape, lambda b, s: (0, 0, 0)),
                  pl.BlockSpec((1, POOL_WIDTH), lambda b, s: (0, 0))],
        out_specs=pl.BlockSpec((1, ts, POOL_WIDTH), lambda b, s: (b, s, 0)),
        out_shape=jax.ShapeDtypeStruct((B, S, POOL_WIDTH), MXU_DTYPE),
        scratch_shapes=[pltpu.VMEM((POOL_HALO, POOL_WIDTH), F32)],
        compiler_params=_params("parallel", "arbitrary"),
        name="pool",
    )(pu3, pool_w, pool_scale_row)


DN_HALO = SUBLANES


def _unit_lower_inverses(lows):
    n = lows[0].shape[0]
    eye = (lax.broadcasted_iota(jnp.int32, (n, n), 0) == lax.broadcasted_iota(jnp.int32, (n, n), 1)).astype(F32)
    lows_b = [_mx(low) for low in lows]
    invs = [eye - low for low in lows]
    powers = [_dot(lb, lb) for lb in lows_b]
    span = 2
    while True:
        powers_b = [_mx(pw) for pw in powers]
        invs = [inv + _dot(_mx(inv), pb) for inv, pb in zip(invs, powers_b)]
        span *= 2
        if span >= n:
            return invs
        powers = [_dot(pb, pb) for pb in powers_b]


def _dn_kernel(qkv_ref, dg_ref, g_ref, gt_ref, cw_ref, nw_ref, o_ref, halo_sc, state_sc):
    c_idx = pl.program_id(0)
    NB, C = qkv_ref.shape[0], qkv_ref.shape[1]

    @pl.when(c_idx == 0)
    def _():
        halo_sc[...] = jnp.zeros(halo_sc.shape, F32)
        state_sc[...] = jnp.zeros(state_sc.shape, F32)

    ri = lax.broadcasted_iota(jnp.int32, (C, C), 0)
    ci = lax.broadcasted_iota(jnp.int32, (C, C), 1)
    causal = ri >= ci
    strict = ri > ci

    chains = [(b, h) for b in range(NB) for h in range(DN_HEADS)]
    qs, ks, vs = [], [], []
    for b in range(NB):
        u = qkv_ref[b]
        ext = jnp.concatenate([halo_sc[b], u], axis=0)
        halo_sc[b] = u[C - DN_HALO:, :]
        conv = u * cw_ref[DN_CONV - 1:DN_CONV, :]
        for d in range(1, DN_CONV):
            conv = conv + pltpu.roll(ext, d, axis=0)[DN_HALO:, :] * cw_ref[DN_CONV - 1 - d:DN_CONV - d, :]
        act = _silu(conv)
        for h in range(DN_HEADS):
            q = act[:, h * DN_HEAD_DIM:(h + 1) * DN_HEAD_DIM]
            k = act[:, DN_WIDTH + h * DN_HEAD_DIM:DN_WIDTH + (h + 1) * DN_HEAD_DIM]
            qs.append(q * lax.rsqrt(jnp.sum(q * q, axis=-1, keepdims=True) + NORM_EPS) * (DN_HEAD_DIM ** -0.5))
            ks.append(k * lax.rsqrt(jnp.sum(k * k, axis=-1, keepdims=True) + NORM_EPS))
            vs.append(act[:, 2 * DN_WIDTH + h * DN_HEAD_DIM:2 * DN_WIDTH + (h + 1) * DN_HEAD_DIM])

    gcs = [g_ref[b, :, DECAY_LANE + h:DECAY_LANE + h + 1] for b, h in chains]
    betas = [g_ref[b, :, BETA_LANE + h:BETA_LANE + h + 1] for b, h in chains]
    gc_rows = [gt_ref[b, DECAY_LANE + h:DECAY_LANE + h + 1, :] for b, h in chains]
    g_lasts = [gc[C - 1:C, :] for gc in gcs]
    decays = [jnp.where(causal, jnp.exp(jnp.where(causal, gc - gr, 0.0)), 0.0) for gc, gr in zip(gcs, gc_rows)]
    kbs = [_mx(k) for k in ks]
    kks = [_dot_nt(kb, kb) for kb in kbs]
    qks = [_dot_nt(_mx(q), kb) for q, kb in zip(qs, kbs)]
    lows = [jnp.where(strict, beta * kk * decay, 0.0) for beta, kk, decay in zip(betas, kks, decays)]
    attns = [_mx(jnp.where(causal, qk * decay, 0.0)) for qk, decay in zip(qks, decays)]
    invs = _unit_lower_inverses(lows)
    e_gcs = [jnp.exp(gc) for gc in gcs]
    rhss = [_mx(jnp.concatenate([v * beta, k * (beta * eg)], axis=1)) for v, k, beta, eg in zip(vs, ks, betas, e_gcs)]
    sols = [_dot(_mx(inv), rhs) for inv, rhs in zip(invs, rhss)]
    q_decs = [_mx(q * eg) for q, eg in zip(qs, e_gcs)]
    k_dec_ts = [_mx((k * jnp.exp(gl - gc)).T) for k, gl, gc in zip(ks, g_lasts, gcs)]

    states = [state_sc[b, h] for b, h in chains]
    states_b = [_mx(st) for st in states]
    u_news = [_mx(sol[:, :DN_HEAD_DIM] - _dot(_mx(sol[:, DN_HEAD_DIM:]), sb)) for sol, sb in zip(sols, states_b)]
    outs = [_dot(qd, sb) + _dot(at, un) for qd, sb, at, un in zip(q_decs, states_b, attns, u_news)]
    for (b, h), st, gl, kt, un in zip(chains, states, g_lasts, k_dec_ts, u_news):
        state_sc[b, h] = st * jnp.exp(gl) + _dot(kt, un)
    for (b, h), o in zip(chains, outs):
        sl = slice(h * DN_HEAD_DIM, (h + 1) * DN_HEAD_DIM)
        o = o * lax.rsqrt(jnp.mean(o * o, axis=-1, keepdims=True) + NORM_EPS) * nw_ref[...]
        o = o * _silu(dg_ref[b, :, sl])
        o_ref[b, :, sl] = o.astype(o_ref.dtype)


def _deltanet(dqkv3, dg3, gates, gates_t, conv_w, norm_w_row):
    B, S, _ = dqkv3.shape
    C = DN_CHUNK
    return pl.pallas_call(
        _dn_kernel,
        grid=(S // C,),
        in_specs=[pl.BlockSpec((B, C, 3 * DN_WIDTH), lambda c: (0, c, 0)),
                  pl.BlockSpec((B, C, DN_WIDTH), lambda c: (0, c, 0)),
                  pl.BlockSpec((B, C, LANES), lambda c: (0, c, 0)),
                  pl.BlockSpec((B, GATE_ROWS, C), lambda c: (0, 0, c)),
                  pl.BlockSpec((DN_CONV, 3 * DN_WIDTH), lambda c: (0, 0)),
                  pl.BlockSpec((1, DN_HEAD_DIM), lambda c: (0, 0))],
        out_specs=pl.BlockSpec((B, C, DN_WIDTH), lambda c: (0, c, 0)),
        out_shape=jax.ShapeDtypeStruct((B, S, DN_WIDTH), MXU_DTYPE),
        scratch_shapes=[pltpu.VMEM((B, DN_HALO, 3 * DN_WIDTH), F32),
                        pltpu.VMEM((B, DN_HEADS, DN_HEAD_DIM, DN_HEAD_DIM), F32)],
        compiler_params=_params("arbitrary"),
        name="deltanet",
    )(dqkv3, dg3, gates, gates_t, conv_w, norm_w_row)


ROUTE_E1, ROUTE_E2, ROUTE_G1, ROUTE_G2 = range(4)
ROUTER_EXPERT_LANE = N_GROUPS


def _merge_kernel(x_ref, ya_ref, yp_ref, yd_ref, p_ref, wgate_ref, wa_ref, wp_ref, wd_ref, wo_ref,
                  g1_ref, b1_ref, wr_ref, br_ref, wpg_ref, wpp_ref,
                  x1_ref, base_ref, route_ref, cnt_ref, cnt_sc):
    tm = x_ref.shape[0]

    @pl.when(pl.program_id(0) == 0)
    def _():
        cnt_sc[...] = jnp.zeros(cnt_sc.shape, F32)

    x = x_ref[...]
    gate = _dot(_mx(x), wgate_ref[...])
    merged = (_sigmoid(gate[:, :D_MODEL]) * _dot(ya_ref[...], wa_ref[...])
              + _sigmoid(gate[:, D_MODEL:2 * D_MODEL]) * _dot(yp_ref[...], wp_ref[...])
              + _sigmoid(gate[:, 2 * D_MODEL:]) * _dot(yd_ref[...], wd_ref[...]))
    mix = _dot(_mx(merged), wo_ref[...])
    x1 = _layer_norm(DEEPNORM_ALPHA * x + mix, g1_ref[...], b1_ref[...])
    x1_ref[...] = x1
    x1b = _mx(x1)
    ple = _sigmoid(_dot(x1b, wpg_ref[...])) * _dot(_mx(p_ref[...]), wpp_ref[...])
    base_ref[...] = DEEPNORM_ALPHA * x1 + ple

    logits = _dot(x1b, wr_ref[...]) + br_ref[...]
    lane = lax.broadcasted_iota(jnp.int32, (tm, LANES), 1)
    ninf = -jnp.inf

    def first_argmax(vals):
        top = jnp.max(vals, axis=-1, keepdims=True)
        idx = jnp.min(jnp.where(vals == top, lane, LANES), axis=-1, keepdims=True)
        return top, idx

    group_logits = jnp.where(lane < N_GROUPS, logits, ninf)
    g_top, g_idx = first_argmax(group_logits)
    p_group = 1.0 / jnp.sum(jnp.exp(group_logits - g_top), axis=-1, keepdims=True)
    lo = ROUTER_EXPERT_LANE + EXPERTS_PER_GROUP * g_idx
    local = jnp.where((lane >= lo) & (lane < lo + EXPERTS_PER_GROUP), logits, ninf)
    top1, idx1 = first_argmax(local)
    top2, idx2 = first_argmax(jnp.where(lane == idx1, ninf, local))
    z = jnp.sum(jnp.exp(local - top1), axis=-1, keepdims=True)
    prob1 = 1.0 / z
    prob2 = jnp.exp(top2 - top1) / z
    gate1 = p_group * prob1 / (prob1 + prob2)
    gate2 = p_group * prob2 / (prob1 + prob2)
    e1 = idx1 - ROUTER_EXPERT_LANE
    e2 = idx2 - ROUTER_EXPERT_LANE
    cnt_sc[...] = cnt_sc[...] + jnp.sum(((lane == e1) | (lane == e2)).astype(F32), axis=0, keepdims=True)
    cnt_ref[...] = jnp.broadcast_to(cnt_sc[...], cnt_ref.shape)
    route = jnp.zeros((tm, LANES), F32)
    for pos, val in ((ROUTE_E1, e1.astype(F32)), (ROUTE_E2, e2.astype(F32)),
                     (ROUTE_G1, gate1), (ROUTE_G2, gate2)):
        route = jnp.where(lane == pos, val, route)
    route_ref[...] = route


def _merge(x2, ya, yp, yd, p2, wgate, wa, wp, wd, wo, g1, b1, wr, br, wpg, wpp):
    T = x2.shape[0]
    tm = TM_MERGE
    row = lambda n: pl.BlockSpec((tm, n), lambda i: (i, 0))
    consts = [wgate, wa, wp, wd, wo, g1, b1, wr, br, wpg, wpp]
    return pl.pallas_call(
        _merge_kernel,
        grid=(T // tm,),
        in_specs=[row(D_MODEL), row(ATTN_WIDTH), row(POOL_WIDTH), row(DN_WIDTH), row(PLE_DIM)]
                 + [_const_spec(c.shape) for c in consts],
        out_specs=[row(D_MODEL), row(D_MODEL), row(LANES), pl.BlockSpec((SUBLANES, LANES), lambda i: (0, 0))],
        out_shape=[jax.ShapeDtypeStruct((T, D_MODEL), F32),
                   jax.ShapeDtypeStruct((T, D_MODEL), F32),
                   jax.ShapeDtypeStruct((T, LANES), F32),
                   jax.ShapeDtypeStruct((SUBLANES, LANES), F32)],
        scratch_shapes=[pltpu.VMEM((1, LANES), F32)],
        compiler_params=_params("arbitrary"),
        name="merge_router",
    )(x2, ya, yp, yd, p2, *consts)


def _dest_kernel(route_ref, cnt_ref, d1_ref, d2_ref, pend_ref, start_sc):
    tm = route_ref.shape[0]
    lane = lax.broadcasted_iota(jnp.int32, (tm, LANES), 1)
    e1 = route_ref[:, ROUTE_E1:ROUTE_E1 + 1].astype(jnp.int32)
    e2 = route_ref[:, ROUTE_E2:ROUTE_E2 + 1].astype(jnp.int32)
    onehot = ((lane == e1) | (lane == e2)).astype(F32)

    @pl.when(pl.program_id(0) == 0)
    def _():
        blk = float(MOE_BLOCK)
        padded = jnp.floor((cnt_ref[...] + (blk - 1.0)) / blk) * blk
        ri = lax.broadcasted_iota(jnp.int32, (LANES, LANES), 0)
        ci = lax.broadcasted_iota(jnp.int32, (LANES, LANES), 1)
        ends = _dot_exact(padded, (ri <= ci).astype(F32))
        start_sc[...] = ends[:1] - padded[:1]
        pend_ref[...] = ends

    ri = lax.broadcasted_iota(jnp.int32, (tm, tm), 0)
    ci = lax.broadcasted_iota(jnp.int32, (tm, tm), 1)
    before = _dot((ri > ci).astype(jnp.bfloat16), onehot.astype(jnp.bfloat16)) + start_sc[...]
    dest = jnp.where(lane == 0, jnp.sum(jnp.where(lane == e1, before, 0.0), axis=-1, keepdims=True),
                     jnp.where(lane == 1, jnp.sum(jnp.where(lane == e2, before, 0.0), axis=-1, keepdims=True), 0.0))
    pick = (lax.broadcasted_iota(jnp.int32, (SUBLANES, LANES), 1)
            == lax.broadcasted_iota(jnp.int32, (SUBLANES, LANES), 0)).astype(F32)
    rows = lax.dot_general(pick, dest, (((1,), (1,)), ((), ())), preferred_element_type=F32,
                           precision=lax.Precision.HIGHEST)
    d1_ref[...] = rows[0:1, :].astype(jnp.int32)
    d2_ref[...] = rows[1:2, :].astype(jnp.int32)
    start_sc[...] = start_sc[...] + jnp.sum(onehot, axis=0, keepdims=True)


def _dest(route, counts):
    T = route.shape[0]
    tm = TM_DEST
    return pl.pallas_call(
        _dest_kernel,
        grid=(T // tm,),
        in_specs=[pl.BlockSpec((tm, LANES), lambda i: (i, 0)),
                  pl.BlockSpec((SUBLANES, LANES), lambda i: (0, 0))],
        out_specs=[pl.BlockSpec((1, tm), lambda i: (0, i)),
                   pl.BlockSpec((1, tm), lambda i: (0, i)),
                   pl.BlockSpec((SUBLANES, LANES), lambda i: (0, 0))],
        out_shape=[jax.ShapeDtypeStruct((1, T), jnp.int32),
                   jax.ShapeDtypeStruct((1, T), jnp.int32),
                   jax.ShapeDtypeStruct((SUBLANES, LANES), F32)],
        scratch_shapes=[pltpu.VMEM((1, LANES), F32)],
        compiler_params=_params("arbitrary"),
        name="moe_dest",
    )(route, counts)


ISSUE_UNROLL = 8


def _row_copy(src, src_row, dst, dst_row, sem):
    return pltpu.make_async_copy(src.at[pl.ds(src_row, 1), :], dst.at[pl.ds(dst_row, 1), :], sem)


def _dispatch_kernel(d1_ref, d2_ref, x_ref, xs_in_ref, xs_ref, sem):
    del xs_in_ref
    tm = x_ref.shape[0]

    def issue(t, carry):
        _row_copy(x_ref, t, xs_ref, d1_ref[0, 0, t], sem.at[0]).start()
        _row_copy(x_ref, t, xs_ref, d2_ref[0, 0, t], sem.at[1]).start()
        return carry

    lax.fori_loop(0, tm, issue, 0, unroll=ISSUE_UNROLL)
    pltpu.make_async_copy(x_ref, xs_ref.at[pl.ds(0, tm), :], sem.at[0]).wait()
    pltpu.make_async_copy(x_ref, xs_ref.at[pl.ds(0, tm), :], sem.at[1]).wait()


def _dispatch(x1, dest1, dest2, n_rows):
    T = x1.shape[0]
    tm = TM_ROWS
    idx_spec = pl.BlockSpec((1, 1, tm), lambda i: (i, 0, 0), memory_space=pltpu.SMEM)
    xs0 = jnp.zeros((n_rows, D_MODEL), F32)
    return pl.pallas_call(
        _dispatch_kernel,
        grid=(T // tm,),
        in_specs=[idx_spec, idx_spec, pl.BlockSpec((tm, D_MODEL), lambda i: (i, 0)),
                  pl.BlockSpec(memory_space=pl.ANY)],
        out_specs=pl.BlockSpec(memory_space=pl.ANY),
        out_shape=jax.ShapeDtypeStruct((n_rows, D_MODEL), F32),
        scratch_shapes=[pltpu.SemaphoreType.DMA((2,))],
        input_output_aliases={3: 0},
        compiler_params=_params("arbitrary"),
        name="moe_dispatch",
    )(dest1.reshape(T // tm, 1, tm), dest2.reshape(T // tm, 1, tm), x1, xs0)


def _experts_kernel(bexp_ref, nreal_ref, xs_ref, wg_ref, wu_ref, wd_ref, ys_ref, wg_sc, wu_sc, wd_sc):
    i = pl.program_id(0)
    changed = (i == 0) | (bexp_ref[i] != bexp_ref[jnp.maximum(i - 1, 0)])

    @pl.when(changed)
    def _():
        wg_sc[...] = _mx(wg_ref[0, 0])
        wu_sc[...] = _mx(wu_ref[0, 0])
        wd_sc[...] = _mx(wd_ref[0, 0])

    @pl.when(i < nreal_ref[0])
    def _():
        xb = _mx(xs_ref[...])
        hid = _silu(_dot(xb, wg_sc[...])) * _dot(xb, wu_sc[...])
        ys_ref[...] = _dot(_mx(hid), wd_sc[...])

    @pl.when(i >= nreal_ref[0])
    def _():
        ys_ref[...] = jnp.zeros(ys_ref.shape, F32)


def _experts(xs, block_expert, n_real, layer, w_gate, w_up, w_down):
    n_rows = xs.shape[0]
    blk = MOE_BLOCK
    grid_spec = pltpu.PrefetchScalarGridSpec(
        num_scalar_prefetch=2,
        grid=(n_rows // blk,),
        in_specs=[pl.BlockSpec((blk, D_MODEL), lambda i, be, nr: (i, 0)),
                  pl.BlockSpec((1, 1, D_MODEL, D_EXPERT), lambda i, be, nr: (layer, be[i], 0, 0)),
                  pl.BlockSpec((1, 1, D_MODEL, D_EXPERT), lambda i, be, nr: (layer, be[i], 0, 0)),
                  pl.BlockSpec((1, 1, D_EXPERT, D_MODEL), lambda i, be, nr: (layer, be[i], 0, 0))],
        out_specs=pl.BlockSpec((blk, D_MODEL), lambda i, be, nr: (i, 0)),
        scratch_shapes=[pltpu.VMEM((D_MODEL, D_EXPERT), MXU_DTYPE),
                        pltpu.VMEM((D_MODEL, D_EXPERT), MXU_DTYPE),
                        pltpu.VMEM((D_EXPERT, D_MODEL), MXU_DTYPE)],
    )
    return pl.pallas_call(
        _experts_kernel,
        grid_spec=grid_spec,
        out_shape=jax.ShapeDtypeStruct((n_rows, D_MODEL), F32),
        compiler_params=_params("arbitrary"),
        name="moe_experts",
    )(block_expert, n_real, xs, w_gate, w_up, w_down)


def _combine_kernel(d1_ref, d2_ref, route_ref, base_ref, ys_ref, g2_ref, b2_ref, o_ref, buf, sem):
    tm = base_ref.shape[0]

    def issue(t, carry):
        _row_copy(ys_ref, d1_ref[0, 0, t], buf.at[0], t, sem.at[0]).start()
        _row_copy(ys_ref, d2_ref[0, 0, t], buf.at[1], t, sem.at[1]).start()
        return carry

    lax.fori_loop(0, tm, issue, 0, unroll=ISSUE_UNROLL)
    pltpu.make_async_copy(ys_ref.at[pl.ds(0, tm), :], buf.at[0], sem.at[0]).wait()
    pltpu.make_async_copy(ys_ref.at[pl.ds(0, tm), :], buf.at[1], sem.at[1]).wait()

    gate1 = route_ref[:, ROUTE_G1:ROUTE_G1 + 1]
    gate2 = route_ref[:, ROUTE_G2:ROUTE_G2 + 1]
    moe = gate1 * buf[0] + gate2 * buf[1]
    o_ref[...] = _layer_norm(base_ref[...] + moe, g2_ref[...], b2_ref[...])


def _combine(dest1, dest2, route, base, ys, g2, b2):
    T = base.shape[0]
    tm = TM_ROWS
    idx_spec = pl.BlockSpec((1, 1, tm), lambda i: (i, 0, 0), memory_space=pltpu.SMEM)
    return pl.pallas_call(
        _combine_kernel,
        grid=(T // tm,),
        in_specs=[idx_spec, idx_spec,
                  pl.BlockSpec((tm, LANES), lambda i: (i, 0)),
                  pl.BlockSpec((tm, D_MODEL), lambda i: (i, 0)),
                  pl.BlockSpec(memory_space=pl.ANY),
                  pl.BlockSpec((1, D_MODEL), lambda i: (0, 0)),
                  pl.BlockSpec((1, D_MODEL), lambda i: (0, 0))],
        out_specs=pl.BlockSpec((tm, D_MODEL), lambda i: (i, 0)),
        out_shape=jax.ShapeDtypeStruct((T, D_MODEL), F32),
        scratch_shapes=[pltpu.VMEM((2, tm, D_MODEL), F32), pltpu.SemaphoreType.DMA((2,))],
        compiler_params=_params("arbitrary"),
        name="moe_combine",
    )(dest1.reshape(T // tm, 1, tm), dest2.reshape(T // tm, 1, tm), route, base, ys, g2, b2)


def _pad_lanes(a):
    return jnp.pad(a, ((0, 0), (0, LANES - a.shape[1])))


def _layer(layer, x2, p2, B, S, w_in, b_forget, pool_w, pool_scale, dn_conv, dn_a_log, dn_dt_bias, dn_norm_w,
           w_br_attn, w_br_pool, w_br_dn, w_out, ln1_g, ln1_b, w_rg, b_rg, w_re, b_re,
           w_exp_gate, w_exp_up, w_exp_down, w_pp, w_pg, ln2_g, ln2_b):
    T = B * S
    o = 0
    cols = {}
    for name, width in (("attn", 3 * ATTN_WIDTH), ("forget", ATTN_HEADS), ("pool", POOL_WIDTH),
                        ("dn", 3 * DN_WIDTH), ("decay", DN_HEADS), ("beta", DN_HEADS),
                        ("dgate", DN_WIDTH), ("merge", 3 * D_MODEL)):
        cols[name] = w_in[:, o:o + width]
        o += width
    w_small = _pad_lanes(jnp.concatenate([cols["forget"], cols["decay"], cols["beta"]], axis=1))
    q_scale = jnp.concatenate([jnp.full((ATTN_WIDTH,), LOG2_E * ATTN_HEAD_DIM ** -0.5, F32),
                               jnp.ones((2 * ATTN_WIDTH,), F32)])
    qkv, pu, dqkv, dg, small = _inproj(x2, _mx(cols["attn"] * q_scale), _mx(cols["pool"]), _mx(cols["dn"]),
                                       _mx(cols["dgate"]), _mx(w_small))

    bias_row = _pad_lanes(jnp.concatenate([b_forget, dn_dt_bias])[None, :])
    alog_row = _pad_lanes(jnp.concatenate([jnp.zeros((ATTN_HEADS,), F32), dn_a_log])[None, :])
    gates, gates_t, q_aug, k_aug, v_aug = _gates(small.reshape(B, S, LANES), bias_row, alog_row,
                                              qkv.reshape(B, S, 3 * ATTN_WIDTH))

    y_attn = _fox(q_aug, k_aug, v_aug)
    y_pool = _pool(pu.reshape(B, S, POOL_WIDTH), _mx(pool_w), pool_scale[None, :])
    y_dn = _deltanet(dqkv.reshape(B, S, 3 * DN_WIDTH), dg.reshape(B, S, DN_WIDTH), gates, gates_t,
                     dn_conv, dn_norm_w[None, :])

    w_router = _pad_lanes(jnp.concatenate([w_rg, w_re], axis=1))
    b_router = _pad_lanes(jnp.concatenate([b_rg, b_re])[None, :])
    x1, base, route, counts = _merge(
        x2, y_attn.reshape(T, ATTN_WIDTH), y_pool.reshape(T, POOL_WIDTH), y_dn.reshape(T, DN_WIDTH), p2,
        _mx(cols["merge"]), _mx(w_br_attn), _mx(w_br_pool), _mx(w_br_dn), _mx(w_out),
        ln1_g[None, :], ln1_b[None, :], _mx(w_router), b_router, _mx(w_pg), _mx(w_pp))

    dest1, dest2, pend = _dest(route, counts)
    blk = MOE_BLOCK
    n_rows = 2 * T + N_EXPERTS * blk
    padded_end = pend[0, :N_EXPERTS].astype(jnp.int32)
    block_start = jnp.arange(n_rows // blk, dtype=jnp.int32) * blk
    block_expert = jnp.minimum(jnp.sum(block_start[:, None] >= padded_end[None, :], axis=1),
                               N_EXPERTS - 1).astype(jnp.int32)
    n_real = (padded_end[-1:] // blk).astype(jnp.int32)

    xs = _dispatch(x1, dest1, dest2, n_rows)
    ys = _experts(xs, block_expert, n_real, layer, w_exp_gate, w_exp_up, w_exp_down)
    return _combine(dest1, dest2, route, base, ys, ln2_g[None, :], ln2_b[None, :])


def kernel(x, p, w_in, b_forget, pool_w, pool_scale, dn_conv, dn_a_log, dn_dt_bias, dn_norm_w, w_br_attn, w_br_pool, w_br_dn, w_out, ln1_g, ln1_b, w_router_group, b_router_group, w_router_expert, b_router_expert, w_exp_gate, w_exp_up, w_exp_down, w_ple_proj, w_ple_gate, ln2_g, ln2_b):
    B, S, _ = x.shape
    x2 = x.reshape(B * S, D_MODEL)
    for i in range(DEPTH):
        x2 = _layer(i, x2, p[i].reshape(B * S, PLE_DIM), B, S, w_in[i], b_forget[i], pool_w[i], pool_scale[i],
                    dn_conv[i], dn_a_log[i], dn_dt_bias[i], dn_norm_w[i], w_br_attn[i], w_br_pool[i],
                    w_br_dn[i], w_out[i], ln1_g[i], ln1_b[i], w_router_group[i], b_router_group[i],
                    w_router_expert[i], b_router_expert[i], w_exp_gate, w_exp_up, w_exp_down,
                    w_ple_proj[i], w_ple_gate[i], ln2_g[i], ln2_b[i])
    return x2.reshape(B, S, D_MODEL)
```

```python
import functools

import numpy as np
import jax
import jax.numpy as jnp
from jax import lax
from jax.experimental import pallas as pl
from jax.experimental.pallas import tpu as pltpu

F32 = jnp.float32
MXU_DTYPE = jnp.bfloat16

D_MODEL = 1024
DEPTH = 4
PLE_DIM = 256
ATTN_HEADS = 8
ATTN_HEAD_DIM = 64
ATTN_WIDTH = ATTN_HEADS * ATTN_HEAD_DIM
POOL_WINDOWS = (2, 4, 8, 16)
POOL_GROUP_DIM = 128
POOL_WIDTH = len(POOL_WINDOWS) * POOL_GROUP_DIM
DN_HEADS = 4
DN_HEAD_DIM = 128
DN_WIDTH = DN_HEADS * DN_HEAD_DIM
DN_CONV = 4
N_GROUPS = 4
EXPERTS_PER_GROUP = 8
N_EXPERTS = N_GROUPS * EXPERTS_PER_GROUP
D_EXPERT = 512
DEEPNORM_ALPHA = (2 * DEPTH) ** 0.25
LN_EPS = 1e-5
NORM_EPS = 1e-6
NEG_INF = -1e30
LOG2_E = 1.4426950408889634

LANES = 128
SUBLANES = 8
VMEM_LIMIT = 56 * 1024 * 1024

FORGET_LANE = 0
DECAY_LANE = ATTN_HEADS
BETA_LANE = ATTN_HEADS + DN_HEADS
GATE_ROWS = 16

TM_PROJ = 256
TQ_ATTN = 512
TS_POOL = 512
DN_CHUNK = 128
TM_MERGE = 256
TM_ROWS = 1024
TM_DEST = 512
MOE_BLOCK = 256
CUM_ROWS = 256


def _dot(a, b):
    return jnp.dot(a, b, preferred_element_type=F32)


def _dot_nt(a, b):
    return lax.dot_general(a, b, (((1,), (1,)), ((), ())), preferred_element_type=F32)


def _dot_exact(a, b):
    return jnp.dot(a, b, preferred_element_type=F32, precision=lax.Precision.HIGHEST)


def _mx(a):
    return a.astype(MXU_DTYPE)


def _sigmoid(z):
    return 1.0 / (1.0 + jnp.exp(-z))


def _silu(z):
    return z * _sigmoid(z)


def _layer_norm(h, g, b):
    mu = jnp.mean(h, axis=-1, keepdims=True)
    hc = h - mu
    var = jnp.mean(hc * hc, axis=-1, keepdims=True)
    return hc * lax.rsqrt(var + LN_EPS) * g + b


def _const_spec(shape):
    zeros = (0,) * len(shape)
    return pl.BlockSpec(shape, lambda *_: zeros, pipeline_mode=pl.Buffered(1))


def _params(*sem):
    return pltpu.CompilerParams(dimension_semantics=sem, vmem_limit_bytes=VMEM_LIMIT)


def _inproj_kernel(x_ref, wa_ref, wp_ref, wd_ref, wg_ref, ws_ref,
                   qkv_ref, pu_ref, dqkv_ref, dg_ref, small_ref):
    xb = _mx(x_ref[...])
    qkv_ref[...] = _dot(xb, wa_ref[...]).astype(qkv_ref.dtype)
    pu_ref[...] = _dot(xb, wp_ref[...]).astype(pu_ref.dtype)
    dqkv_ref[...] = _dot(xb, wd_ref[...]).astype(dqkv_ref.dtype)
    dg_ref[...] = _dot(xb, wg_ref[...]).astype(dg_ref.dtype)
    small_ref[...] = _dot(xb, ws_ref[...])


def _inproj(x2, wa, wp, wd, wg, ws):
    T = x2.shape[0]
    tm = TM_PROJ
    row = lambda n: pl.BlockSpec((tm, n), lambda i: (i, 0))
    return pl.pallas_call(
        _inproj_kernel,
        grid=(T // tm,),
        in_specs=[row(D_MODEL), _const_spec(wa.shape), _const_spec(wp.shape), _const_spec(wd.shape),
                  _const_spec(wg.shape), _const_spec(ws.shape)],
        out_specs=[row(3 * ATTN_WIDTH), row(POOL_WIDTH), row(3 * DN_WIDTH), row(DN_WIDTH), row(LANES)],
        out_shape=[jax.ShapeDtypeStruct((T, 3 * ATTN_WIDTH), MXU_DTYPE),
                   jax.ShapeDtypeStruct((T, POOL_WIDTH), MXU_DTYPE),
                   jax.ShapeDtypeStruct((T, 3 * DN_WIDTH), MXU_DTYPE),
                   jax.ShapeDtypeStruct((T, DN_WIDTH), MXU_DTYPE),
                   jax.ShapeDtypeStruct((T, LANES), F32)],
        compiler_params=_params("parallel"),
        name="inproj",
    )(x2, wa, wp, wd, wg, ws)


BIAS_SPLIT_LANES = 3 * ATTN_HEADS
BIAS_ONES_LANE = BIAS_SPLIT_LANES


def _sum_lane(half):
    return (1 - half) * ATTN_HEAD_DIM


def _bias_placement():
    pq = np.zeros((LANES, ATTN_HEADS * LANES), np.float32)
    pk = np.zeros((LANES, ATTN_HEADS * LANES), np.float32)
    for h in range(ATTN_HEADS):
        base = h * LANES + (1 - h % 2) * ATTN_HEAD_DIM
        for part in range(3):
            pq[part * ATTN_HEADS + h, base + part] = 1.0
            pq[BIAS_ONES_LANE, base + 3 + part] = 1.0
            pk[BIAS_ONES_LANE, base + part] = 1.0
            pk[part * ATTN_HEADS + h, base + 3 + part] = -1.0
    return pq, pk


def _gates_kernel(small_ref, bias_ref, alog_ref, qkv_ref, pq_ref, pk_ref,
                  g_ref, gt_ref, qa_ref, ka_ref, va_ref, carry_sc):
    R = small_ref.shape[1]

    @pl.when(pl.program_id(1) == 0)
    def _():
        carry_sc[...] = jnp.zeros(carry_sc.shape, F32)

    ri = lax.broadcasted_iota(jnp.int32, (R, R), 0)
    ci = lax.broadcasted_iota(jnp.int32, (R, R), 1)
    tri_full = (ri >= ci).astype(F32)
    tri_chunk = ((ri >= ci) & (ri // DN_CHUNK == ci // DN_CHUNK)).astype(F32)
    lane = lax.broadcasted_iota(jnp.int32, (R, LANES), 1)
    is_forget = lane < DECAY_LANE
    is_decay = (lane >= DECAY_LANE) & (lane < BETA_LANE)

    z = small_ref[0] + bias_ref[...]
    soft = jnp.log(1.0 + jnp.exp(-jnp.abs(z)))
    log_f = jnp.minimum(z, 0.0) - soft
    decay = -jnp.exp(alog_ref[...]) * (jnp.maximum(z, 0.0) + soft)
    vals = jnp.where(is_forget, log_f, jnp.where(is_decay, decay, _sigmoid(z)))
    cum_full = _dot_exact(tri_full, vals) + carry_sc[...]
    cum_chunk = _dot_exact(tri_chunk, vals)
    carry_sc[...] = cum_full[R - 1:R, :]
    out = jnp.where(is_forget, cum_full, jnp.where(is_decay, cum_chunk, vals))
    g_ref[0] = out
    gt_ref[0] = out.T[:GATE_ROWS, :]

    c = jnp.where(is_forget, out, 0.0) * LOG2_E
    hi = _mx(c).astype(F32)
    mid = _mx(c - hi).astype(F32)
    lo = _mx(c - hi - mid).astype(F32)
    split = (hi + pltpu.roll(mid, ATTN_HEADS, axis=1) + pltpu.roll(lo, 2 * ATTN_HEADS, axis=1)
             + jnp.where(lane == BIAS_ONES_LANE, 1.0, 0.0))
    split_b = _mx(split)
    extra_q = _dot(split_b, pq_ref[...]).astype(qa_ref.dtype)
    extra_k = _dot(split_b, pk_ref[...]).astype(ka_ref.dtype)
    lane1 = lax.broadcasted_iota(jnp.int32, (1, LANES), 1)
    for pair in range(ATTN_HEADS // 2):
        qp = qkv_ref[0, :, pair * LANES:(pair + 1) * LANES]
        kp = qkv_ref[0, :, ATTN_WIDTH + pair * LANES:ATTN_WIDTH + (pair + 1) * LANES]
        vp = qkv_ref[0, :, 2 * ATTN_WIDTH + pair * LANES:2 * ATTN_WIDTH + (pair + 1) * LANES]
        for half in range(2):
            h = 2 * pair + half
            own = (lane1 >= half * ATTN_HEAD_DIM) & (lane1 < (half + 1) * ATTN_HEAD_DIM)
            ones = jnp.where(lane1 == _sum_lane(half), 1.0, 0.0).astype(va_ref.dtype)
            qa_ref[0, h] = jnp.where(own, qp, extra_q[:, h * LANES:(h + 1) * LANES])
            ka_ref[0, h] = jnp.where(own, kp, extra_k[:, h * LANES:(h + 1) * LANES])
            va_ref[0, h] = jnp.where(own, vp, ones)


def _gates(small3, bias_row, alog_row, qkv3):
    B, S, _ = small3.shape
    R = CUM_ROWS
    pq, pk = _bias_placement()
    pq = jnp.asarray(pq, MXU_DTYPE)
    pk = jnp.asarray(pk, MXU_DTYPE)
    H = ATTN_HEADS
    return pl.pallas_call(
        _gates_kernel,
        grid=(B, S // R),
        in_specs=[pl.BlockSpec((1, R, LANES), lambda b, r: (b, r, 0)),
                  pl.BlockSpec((1, LANES), lambda b, r: (0, 0)),
                  pl.BlockSpec((1, LANES), lambda b, r: (0, 0)),
                  pl.BlockSpec((1, R, 3 * ATTN_WIDTH), lambda b, r: (b, r, 0)),
                  pl.BlockSpec(pq.shape, lambda b, r: (0, 0)),
                  pl.BlockSpec(pk.shape, lambda b, r: (0, 0))],
        out_specs=[pl.BlockSpec((1, R, LANES), lambda b, r: (b, r, 0)),
                   pl.BlockSpec((1, GATE_ROWS, R), lambda b, r: (b, 0, r)),
                   pl.BlockSpec((1, H, R, LANES), lambda b, r: (b, 0, r, 0)),
                   pl.BlockSpec((1, H, R, LANES), lambda b, r: (b, 0, r, 0)),
                   pl.BlockSpec((1, H, R, LANES), lambda b, r: (b, 0, r, 0))],
        out_shape=[jax.ShapeDtypeStruct((B, S, LANES), F32),
                   jax.ShapeDtypeStruct((B, GATE_ROWS, S), F32),
                   jax.ShapeDtypeStruct((B, H, S, LANES), MXU_DTYPE),
                   jax.ShapeDtypeStruct((B, H, S, LANES), MXU_DTYPE),
                   jax.ShapeDtypeStruct((B, H, S, LANES), MXU_DTYPE)],
        scratch_shapes=[pltpu.VMEM((1, LANES), F32)],
        compiler_params=_params("parallel", "arbitrary"),
        name="gates",
    )(small3, bias_row, alog_row, qkv3, pq, pk)


def _fox_kernel(qi_ref, kj_ref, q_ref, k_ref, v_ref, o_ref, m_sc, acc_sc):
    step = pl.program_id(1)
    i = qi_ref[step]
    j = kj_ref[step]
    tq = q_ref.shape[2]
    tk = k_ref.shape[2]
    reps = tk // LANES

    @pl.when(j == 0)
    def _():
        m_sc[...] = jnp.full(m_sc.shape, NEG_INF, F32)
        acc_sc[...] = jnp.zeros(acc_sc.shape, F32)

    def sweep(masked):
        if masked:
            rows = lax.broadcasted_iota(jnp.int32, (tq, tk), 0)
            cols = lax.broadcasted_iota(jnp.int32, (tq, tk), 1)
            keep = cols <= rows
        for h in range(ATTN_HEADS):
            s = _dot_nt(q_ref[0, h], k_ref[0, h])
            if masked:
                s = jnp.where(keep, s, NEG_INF)
            m_prev = m_sc[h]
            m_next = jnp.maximum(m_prev, jnp.max(s, axis=1)[:, None])
            p = jnp.exp2(s - jnp.concatenate([m_next] * reps, axis=1))
            m_sc[h] = m_next
            acc_sc[h] = jnp.exp2(m_prev - m_next) * acc_sc[h] + _dot(_mx(p), v_ref[0, h])

    @pl.when(j < i)
    def _():
        sweep(False)

    @pl.when(j == i)
    def _():
        sweep(True)
        lane = lax.broadcasted_iota(jnp.int32, (1, LANES), 1)
        for pair in range(ATTN_HEADS // 2):
            halves = []
            for half in range(2):
                acc = acc_sc[2 * pair + half]
                halves.append(acc / acc[:, _sum_lane(half):_sum_lane(half) + 1])
            out = jnp.where(lane < ATTN_HEAD_DIM, halves[0], halves[1])
            o_ref[0, :, pair * LANES:(pair + 1) * LANES] = out.astype(o_ref.dtype)


def _fox(q_aug, k_aug, v_aug):
    B, H, S, _ = q_aug.shape
    t = TQ_ATTN
    n = S // t
    pairs = [(i, j) for i in range(n) for j in range(i + 1)]
    qi = jnp.asarray([i for i, _ in pairs], jnp.int32)
    kj = jnp.asarray([j for _, j in pairs], jnp.int32)
    grid_spec = pltpu.PrefetchScalarGridSpec(
        num_scalar_prefetch=2,
        grid=(B, len(pairs)),
        in_specs=[pl.BlockSpec((1, H, t, LANES), lambda b, s, qi, kj: (b, 0, qi[s], 0)),
                  pl.BlockSpec((1, H, t, LANES), lambda b, s, qi, kj: (b, 0, kj[s], 0)),
                  pl.BlockSpec((1, H, t, LANES), lambda b, s, qi, kj: (b, 0, kj[s], 0))],
        out_specs=pl.BlockSpec((1, t, ATTN_WIDTH), lambda b, s, qi, kj: (b, qi[s], 0)),
        scratch_shapes=[pltpu.VMEM((ATTN_HEADS, t, LANES), F32),
                        pltpu.VMEM((ATTN_HEADS, t, LANES), F32)],
    )
    return pl.pallas_call(
        _fox_kernel,
        grid_spec=grid_spec,
        out_shape=jax.ShapeDtypeStruct((B, S, ATTN_WIDTH), MXU_DTYPE),
        compiler_params=_params("parallel", "arbitrary"),
        name="fox_attention",
    )(qi, kj, q_aug, k_aug, v_aug)


POOL_HALO = 16


def _pool_kernel(u_ref, w_ref, scale_ref, y_ref, halo_sc):
    s_idx = pl.program_id(1)
    ts = u_ref.shape[1]

    @pl.when(s_idx == 0)
    def _():
        halo_sc[...] = jnp.zeros(halo_sc.shape, F32)

    u = u_ref[0].astype(F32)
    ext = jnp.concatenate([halo_sc[...], u], axis=0)
    halo_sc[...] = u[ts - POOL_HALO:, :]
    pos = (lax.broadcasted_iota(jnp.int32, (ts, 1), 0) + s_idx * ts + 1).astype(F32)
    for g, w in enumerate(POOL_WINDOWS):
        sl = slice(g * POOL_GROUP_DIM, (g + 1) * POOL_GROUP_DIM)
        acc = ext[:, sl]
        span = 1
        while span < w:
            acc = acc + pltpu.roll(acc, span, axis=0)
            span *= 2
        mean = acc[POOL_HALO:, :] / jnp.minimum(pos, float(w))
        d = mean - u[:, sl]
        y = _dot(_mx(d), w_ref[g]) * scale_ref[:, sl]
        y_ref[0, :, sl] = y.astype(y_ref.dtype)


def _pool(pu3, pool_w, pool_scale_row):
    B, S, _ = pu3.shape
    ts = TS_POOL
    return pl.pallas_call(
        _pool_kernel,
        grid=(B, S // ts),
        in_specs=[pl.BlockSpec((1, ts, POOL_WIDTH), lambda b, s: (b, s, 0)),
                  pl.BlockSpec(pool_w.shape, lambda b, s: (0, 0, 0)),
                  pl.BlockSpec((1, POOL_WIDTH), lambda b, s: (0, 0))],
        out_specs=pl.BlockSpec((1, ts, POOL_WIDTH), lambda b, s: (b, s, 0)),
        out_shape=jax.ShapeDtypeStruct((B, S, POOL_WIDTH), MXU_DTYPE),
        scratch_shapes=[pltpu.VMEM((POOL_HALO, POOL_WIDTH), F32)],
        compiler_params=_params("parallel", "arbitrary"),
        name="pool",
    )(pu3, pool_w, pool_scale_row)


DN_HALO = SUBLANES


def _unit_lower_inverses(lows):
    n = lows[0].shape[0]
    eye = (lax.broadcasted_iota(jnp.int32, (n, n), 0) == lax.broadcasted_iota(jnp.int32, (n, n), 1)).astype(F32)
    lows_b = [_mx(low) for low in lows]
    invs = [eye - low for low in lows]
    powers = [_dot(lb, lb) for lb in lows_b]
    span = 2
    while True:
        powers_b = [_mx(pw) for pw in powers]
        invs = [inv + _dot(_mx(inv), pb) for inv, pb in zip(invs, powers_b)]
        span *= 2
        if span >= n:
            return invs
        powers = [_dot(pb, pb) for pb in powers_b]


def _dn_kernel(qkv_ref, dg_ref, g_ref, gt_ref, cw_ref, nw_ref, o_ref, halo_sc, state_sc):
    c_idx = pl.program_id(0)
    NB, C = qkv_ref.shape[0], qkv_ref.shape[1]

    @pl.when(c_idx == 0)
    def _():
        halo_sc[...] = jnp.zeros(halo_sc.shape, F32)
        state_sc[...] = jnp.zeros(state_sc.shape, F32)

    ri = lax.broadcasted_iota(jnp.int32, (C, C), 0)
    ci = lax.broadcasted_iota(jnp.int32, (C, C), 1)
    causal = ri >= ci
    strict = ri > ci

    chains = [(b, h) for b in range(NB) for h in range(DN_HEADS)]
    qs, ks, vs = [], [], []
    for b in range(NB):
        u = qkv_ref[b].astype(F32)
        ext = jnp.concatenate([halo_sc[b], u], axis=0)
        halo_sc[b] = u[C - DN_HALO:, :]
        conv = u * cw_ref[DN_CONV - 1:DN_CONV, :]
        for d in range(1, DN_CONV):
            conv = conv + pltpu.roll(ext, d, axis=0)[DN_HALO:, :] * cw_ref[DN_CONV - 1 - d:DN_CONV - d, :]
        act = _silu(conv)
        for h in range(DN_HEADS):
            q = act[:, h * DN_HEAD_DIM:(h + 1) * DN_HEAD_DIM]
            k = act[:, DN_WIDTH + h * DN_HEAD_DIM:DN_WIDTH + (h + 1) * DN_HEAD_DIM]
            qs.append(q * lax.rsqrt(jnp.sum(q * q, axis=-1, keepdims=True) + NORM_EPS) * (DN_HEAD_DIM ** -0.5))
            ks.append(k * lax.rsqrt(jnp.sum(k * k, axis=-1, keepdims=True) + NORM_EPS))
            vs.append(act[:, 2 * DN_WIDTH + h * DN_HEAD_DIM:2 * DN_WIDTH + (h + 1) * DN_HEAD_DIM])

    gcs = [g_ref[b, :, DECAY_LANE + h:DECAY_LANE + h + 1] for b, h in chains]
    betas = [g_ref[b, :, BETA_LANE + h:BETA_LANE + h + 1] for b, h in chains]
    gc_rows = [gt_ref[b, DECAY_LANE + h:DECAY_LANE + h + 1, :] for b, h in chains]
    g_lasts = [gc[C - 1:C, :] for gc in gcs]
    decays = [jnp.where(causal, jnp.exp(jnp.where(causal, gc - gr, 0.0)), 0.0) for gc, gr in zip(gcs, gc_rows)]
    kbs = [_mx(k) for k in ks]
    kks = [_dot_nt(kb, kb) for kb in kbs]
    qks = [_dot_nt(_mx(q), kb) for q, kb in zip(qs, kbs)]
    lows = [jnp.where(strict, beta * kk * decay, 0.0) for beta, kk, decay in zip(betas, kks, decays)]
    attns = [_mx(jnp.where(causal, qk * decay, 0.0)) for qk, decay in zip(qks, decays)]
    invs = _unit_lower_inverses(lows)
    e_gcs = [jnp.exp(gc) for gc in gcs]
    rhss = [_mx(jnp.concatenate([v * beta, k * (beta * eg)], axis=1)) for v, k, beta, eg in zip(vs, ks, betas, e_gcs)]
    sols = [_dot(_mx(inv), rhs) for inv, rhs in zip(invs, rhss)]
    q_decs = [_mx(q * eg) for q, eg in zip(qs, e_gcs)]
    k_dec_ts = [_mx((k * jnp.exp(gl - gc)).T) for k, gl, gc in zip(ks, g_lasts, gcs)]

    states = [state_sc[b, h] for b, h in chains]
    states_b = [_mx(st) for st in states]
    u_news = [_mx(sol[:, :DN_HEAD_DIM] - _dot(_mx(sol[:, DN_HEAD_DIM:]), sb)) for sol, sb in zip(sols, states_b)]
    outs = [_dot(qd, sb) + _dot(at, un) for qd, sb, at, un in zip(q_decs, states_b, attns, u_news)]
    for (b, h), st, gl, kt, un in zip(chains, states, g_lasts, k_dec_ts, u_news):
        state_sc[b, h] = st * jnp.exp(gl) + _dot(kt, un)
    for (b, h), o in zip(chains, outs):
        sl = slice(h * DN_HEAD_DIM, (h + 1) * DN_HEAD_DIM)
        o = o * lax.rsqrt(jnp.mean(o * o, axis=-1, keepdims=True) + NORM_EPS) * nw_ref[...]
        o = o * _silu(dg_ref[b, :, sl].astype(F32))
        o_ref[b, :, sl] = o.astype(o_ref.dtype)


def _deltanet(dqkv3, dg3, gates, gates_t, conv_w, norm_w_row):
    B, S, _ = dqkv3.shape
    C = DN_CHUNK
    return pl.pallas_call(
        _dn_kernel,
        grid=(S // C,),
        in_specs=[pl.BlockSpec((B, C, 3 * DN_WIDTH), lambda c: (0, c, 0)),
                  pl.BlockSpec((B, C, DN_WIDTH), lambda c: (0, c, 0)),
                  pl.BlockSpec((B, C, LANES), lambda c: (0, c, 0)),
                  pl.BlockSpec((B, GATE_ROWS, C), lambda c: (0, 0, c)),
                  pl.BlockSpec((DN_CONV, 3 * DN_WIDTH), lambda c: (0, 0)),
                  pl.BlockSpec((1, DN_HEAD_DIM), lambda c: (0, 0))],
        out_specs=pl.BlockSpec((B, C, DN_WIDTH), lambda c: (0, c, 0)),
        out_shape=jax.ShapeDtypeStruct((B, S, DN_WIDTH), MXU_DTYPE),
        scratch_shapes=[pltpu.VMEM((B, DN_HALO, 3 * DN_WIDTH), F32),
                        pltpu.VMEM((B, DN_HEADS, DN_HEAD_DIM, DN_HEAD_DIM), F32)],
        compiler_params=_params("arbitrary"),
        name="deltanet",
    )(dqkv3, dg3, gates, gates_t, conv_w, norm_w_row)


ROUTE_E1, ROUTE_E2, ROUTE_G1, ROUTE_G2 = range(4)
ROUTER_EXPERT_LANE = N_GROUPS


def _merge_kernel(x_ref, ya_ref, yp_ref, yd_ref, p_ref, wgate_ref, wa_ref, wp_ref, wd_ref, wo_ref,
                  g1_ref, b1_ref, wr_ref, br_ref, wpg_ref, wpp_ref,
                  x1_ref, base_ref, route_ref, cnt_ref, cnt_sc):
    tm = x_ref.shape[0]

    @pl.when(pl.program_id(0) == 0)
    def _():
        cnt_sc[...] = jnp.zeros(cnt_sc.shape, F32)

    x = x_ref[...]
    gate = _dot(_mx(x), wgate_ref[...])
    merged = (_sigmoid(gate[:, :D_MODEL]) * _dot(ya_ref[...], wa_ref[...])
              + _sigmoid(gate[:, D_MODEL:2 * D_MODEL]) * _dot(yp_ref[...], wp_ref[...])
              + _sigmoid(gate[:, 2 * D_MODEL:]) * _dot(yd_ref[...], wd_ref[...]))
    mix = _dot(_mx(merged), wo_ref[...])
    x1 = _layer_norm(DEEPNORM_ALPHA * x + mix, g1_ref[...], b1_ref[...])
    x1_ref[...] = x1
    x1b = _mx(x1)
    ple = _sigmoid(_dot(x1b, wpg_ref[...])) * _dot(_mx(p_ref[...]), wpp_ref[...])
    base_ref[...] = DEEPNORM_ALPHA * x1 + ple

    logits = _dot(x1b, wr_ref[...]) + br_ref[...]
    lane = lax.broadcasted_iota(jnp.int32, (tm, LANES), 1)
    ninf = -jnp.inf

    def first_argmax(vals):
        top = jnp.max(vals, axis=-1, keepdims=True)
        idx = jnp.min(jnp.where(vals == top, lane, LANES), axis=-1, keepdims=True)
        return top, idx

    group_logits = jnp.where(lane < N_GROUPS, logits, ninf)
    g_top, g_idx = first_argmax(group_logits)
    p_group = 1.0 / jnp.sum(jnp.exp(group_logits - g_top), axis=-1, keepdims=True)
    lo = ROUTER_EXPERT_LANE + EXPERTS_PER_GROUP * g_idx
    local = jnp.where((lane >= lo) & (lane < lo + EXPERTS_PER_GROUP), logits, ninf)
    top1, idx1 = first_argmax(local)
    top2, idx2 = first_argmax(jnp.where(lane == idx1, ninf, local))
    z = jnp.sum(jnp.exp(local - top1), axis=-1, keepdims=True)
    prob1 = 1.0 / z
    prob2 = jnp.exp(top2 - top1) / z
    gate1 = p_group * prob1 / (prob1 + prob2)
    gate2 = p_group * prob2 / (prob1 + prob2)
    e1 = idx1 - ROUTER_EXPERT_LANE
    e2 = idx2 - ROUTER_EXPERT_LANE
    cnt_sc[...] = cnt_sc[...] + jnp.sum(((lane == e1) | (lane == e2)).astype(F32), axis=0, keepdims=True)
    cnt_ref[...] = jnp.broadcast_to(cnt_sc[...], cnt_ref.shape)
    route = jnp.zeros((tm, LANES), F32)
    for pos, val in ((ROUTE_E1, e1.astype(F32)), (ROUTE_E2, e2.astype(F32)),
                     (ROUTE_G1, gate1), (ROUTE_G2, gate2)):
        route = jnp.where(lane == pos, val, route)
    route_ref[...] = route


def _merge(x2, ya, yp, yd, p2, wgate, wa, wp, wd, wo, g1, b1, wr, br, wpg, wpp):
    T = x2.shape[0]
    tm = TM_MERGE
    row = lambda n: pl.BlockSpec((tm, n), lambda i: (i, 0))
    consts = [wgate, wa, wp, wd, wo, g1, b1, wr, br, wpg, wpp]
    return pl.pallas_call(
        _merge_kernel,
        grid=(T // tm,),
        in_specs=[row(D_MODEL), row(ATTN_WIDTH), row(POOL_WIDTH), row(DN_WIDTH), row(PLE_DIM)]
                 + [_const_spec(c.shape) for c in consts],
        out_specs=[row(D_MODEL), row(D_MODEL), row(LANES), pl.BlockSpec((SUBLANES, LANES), lambda i: (0, 0))],
        out_shape=[jax.ShapeDtypeStruct((T, D_MODEL), F32),
                   jax.ShapeDtypeStruct((T, D_MODEL), F32),
                   jax.ShapeDtypeStruct((T, LANES), F32),
                   jax.ShapeDtypeStruct((SUBLANES, LANES), F32)],
        scratch_shapes=[pltpu.VMEM((1, LANES), F32)],
        compiler_params=_params("arbitrary"),
        name="merge_router",
    )(x2, ya, yp, yd, p2, *consts)


def _dest_kernel(route_ref, cnt_ref, d1_ref, d2_ref, pend_ref, start_sc):
    tm = route_ref.shape[0]
    lane = lax.broadcasted_iota(jnp.int32, (tm, LANES), 1)
    e1 = route_ref[:, ROUTE_E1:ROUTE_E1 + 1].astype(jnp.int32)
    e2 = route_ref[:, ROUTE_E2:ROUTE_E2 + 1].astype(jnp.int32)
    onehot = ((lane == e1) | (lane == e2)).astype(F32)

    @pl.when(pl.program_id(0) == 0)
    def _():
        blk = float(MOE_BLOCK)
        padded = jnp.floor((cnt_ref[...] + (blk - 1.0)) / blk) * blk
        ri = lax.broadcasted_iota(jnp.int32, (LANES, LANES), 0)
        ci = lax.broadcasted_iota(jnp.int32, (LANES, LANES), 1)
        ends = _dot_exact(padded, (ri <= ci).astype(F32))
        start_sc[...] = ends[:1] - padded[:1]
        pend_ref[...] = ends

    ri = lax.broadcasted_iota(jnp.int32, (tm, tm), 0)
    ci = lax.broadcasted_iota(jnp.int32, (tm, tm), 1)
    before = _dot((ri > ci).astype(jnp.bfloat16), onehot.astype(jnp.bfloat16)) + start_sc[...]
    dest = jnp.where(lane == 0, jnp.sum(jnp.where(lane == e1, before, 0.0), axis=-1, keepdims=True),
                     jnp.where(lane == 1, jnp.sum(jnp.where(lane == e2, before, 0.0), axis=-1, keepdims=True), 0.0))
    pick = (lax.broadcasted_iota(jnp.int32, (SUBLANES, LANES), 1)
            == lax.broadcasted_iota(jnp.int32, (SUBLANES, LANES), 0)).astype(F32)
    rows = lax.dot_general(pick, dest, (((1,), (1,)), ((), ())), preferred_element_type=F32,
                           precision=lax.Precision.HIGHEST)
    d1_ref[...] = rows[0:1, :].astype(jnp.int32)
    d2_ref[...] = rows[1:2, :].astype(jnp.int32)
    start_sc[...] = start_sc[...] + jnp.sum(onehot, axis=0, keepdims=True)


def _dest(route, counts):
    T = route.shape[0]
    tm = TM_DEST
    return pl.pallas_call(
        _dest_kernel,
        grid=(T // tm,),
        in_specs=[pl.BlockSpec((tm, LANES), lambda i: (i, 0)),
                  pl.BlockSpec((SUBLANES, LANES), lambda i: (0, 0))],
        out_specs=[pl.BlockSpec((1, tm), lambda i: (0, i)),
                   pl.BlockSpec((1, tm), lambda i: (0, i)),
                   pl.BlockSpec((SUBLANES, LANES), lambda i: (0, 0))],
        out_shape=[jax.ShapeDtypeStruct((1, T), jnp.int32),
                   jax.ShapeDtypeStruct((1, T), jnp.int32),
                   jax.ShapeDtypeStruct((SUBLANES, LANES), F32)],
        scratch_shapes=[pltpu.VMEM((1, LANES), F32)],
        compiler_params=_params("arbitrary"),
        name="moe_dest",
    )(route, counts)


ISSUE_UNROLL = 8


def _row_copy(src, src_row, dst, dst_row, sem):
    return pltpu.make_async_copy(src.at[pl.ds(src_row, 1), :], dst.at[pl.ds(dst_row, 1), :], sem)


def _dispatch_kernel(pend_ref, d1_ref, d2_ref, x_ref, xs_ref, zero_sc, sem, zsem):
    tm = x_ref.shape[0]
    n_blocks = xs_ref.shape[0] // MOE_BLOCK
    used_blocks = pend_ref[N_EXPERTS - 1] // MOE_BLOCK

    def fill(block):
        first = pl.multiple_of(block * MOE_BLOCK, MOE_BLOCK)
        return pltpu.make_async_copy(zero_sc, xs_ref.at[pl.ds(first, MOE_BLOCK), :], zsem)

    def fills():
        for e in range(N_EXPERTS):
            yield pend_ref[e] // MOE_BLOCK - 1, pend_ref[e] >= MOE_BLOCK
        for t in range(N_EXPERTS):
            yield used_blocks + t, used_blocks + t < n_blocks

    @pl.when(pl.program_id(0) == 0)
    def _():
        zero_sc[...] = jnp.zeros(zero_sc.shape, F32)
        for block, exists in fills():
            @pl.when(exists)
            def _():
                fill(block).start()
        for block, exists in fills():
            @pl.when(exists)
            def _():
                fill(block).wait()

    def issue(t, carry):
        _row_copy(x_ref, t, xs_ref, d1_ref[0, 0, t], sem.at[0]).start()
        _row_copy(x_ref, t, xs_ref, d2_ref[0, 0, t], sem.at[1]).start()
        return carry

    lax.fori_loop(0, tm, issue, 0, unroll=ISSUE_UNROLL)
    pltpu.make_async_copy(x_ref, xs_ref.at[pl.ds(0, tm), :], sem.at[0]).wait()
    pltpu.make_async_copy(x_ref, xs_ref.at[pl.ds(0, tm), :], sem.at[1]).wait()


def _dispatch(x1, dest1, dest2, padded_end, n_rows):
    T = x1.shape[0]
    tm = TM_ROWS
    idx_spec = pl.BlockSpec((1, 1, tm), lambda i, pe: (i, 0, 0), memory_space=pltpu.SMEM)
    grid_spec = pltpu.PrefetchScalarGridSpec(
        num_scalar_prefetch=1,
        grid=(T // tm,),
        in_specs=[idx_spec, idx_spec, pl.BlockSpec((tm, D_MODEL), lambda i, pe: (i, 0))],
        out_specs=pl.BlockSpec(memory_space=pl.ANY),
        scratch_shapes=[pltpu.VMEM((MOE_BLOCK, D_MODEL), F32), pltpu.SemaphoreType.DMA((2,)),
                        pltpu.SemaphoreType.DMA(())],
    )
    return pl.pallas_call(
        _dispatch_kernel,
        grid_spec=grid_spec,
        out_shape=jax.ShapeDtypeStruct((n_rows, D_MODEL), F32),
        compiler_params=_params("arbitrary"),
        name="moe_dispatch",
    )(padded_end, dest1.reshape(T // tm, 1, tm), dest2.reshape(T // tm, 1, tm), x1)


def _experts_kernel(bexp_ref, nreal_ref, xs_ref, wg_ref, wu_ref, wd_ref, ys_ref, wg_sc, wu_sc, wd_sc):
    i = pl.program_id(0)
    changed = (i == 0) | (bexp_ref[i] != bexp_ref[jnp.maximum(i - 1, 0)])

    @pl.when(changed)
    def _():
        wg_sc[...] = _mx(wg_ref[0, 0])
        wu_sc[...] = _mx(wu_ref[0, 0])
        wd_sc[...] = _mx(wd_ref[0, 0])

    @pl.when(i < nreal_ref[0])
    def _():
        xb = _mx(xs_ref[...])
        hid = _silu(_dot(xb, wg_sc[...])) * _dot(xb, wu_sc[...])
        ys_ref[...] = _dot(_mx(hid), wd_sc[...])

    @pl.when(i >= nreal_ref[0])
    def _():
        ys_ref[...] = jnp.zeros(ys_ref.shape, F32)


def _experts(xs, block_expert, n_real, layer, w_gate, w_up, w_down):
    n_rows = xs.shape[0]
    blk = MOE_BLOCK
    grid_spec = pltpu.PrefetchScalarGridSpec(
        num_scalar_prefetch=2,
        grid=(n_rows // blk,),
        in_specs=[pl.BlockSpec((blk, D_MODEL), lambda i, be, nr: (jnp.minimum(i, nr[0] - 1), 0)),
                  pl.BlockSpec((1, 1, D_MODEL, D_EXPERT), lambda i, be, nr: (layer, be[i], 0, 0)),
                  pl.BlockSpec((1, 1, D_MODEL, D_EXPERT), lambda i, be, nr: (layer, be[i], 0, 0)),
                  pl.BlockSpec((1, 1, D_EXPERT, D_MODEL), lambda i, be, nr: (layer, be[i], 0, 0))],
        out_specs=pl.BlockSpec((blk, D_MODEL), lambda i, be, nr: (i, 0)),
        scratch_shapes=[pltpu.VMEM((D_MODEL, D_EXPERT), MXU_DTYPE),
                        pltpu.VMEM((D_MODEL, D_EXPERT), MXU_DTYPE),
                        pltpu.VMEM((D_EXPERT, D_MODEL), MXU_DTYPE)],
    )
    return pl.pallas_call(
        _experts_kernel,
        grid_spec=grid_spec,
        out_shape=jax.ShapeDtypeStruct((n_rows, D_MODEL), F32),
        compiler_params=_params("arbitrary"),
        name="moe_experts",
    )(block_expert, n_real, xs, w_gate, w_up, w_down)


def _combine_kernel(d1_ref, d2_ref, route_ref, base_ref, ys_ref, g2_ref, b2_ref, o_ref, buf, sem):
    tm = base_ref.shape[0]

    def issue(t, carry):
        _row_copy(ys_ref, d1_ref[0, 0, t], buf.at[0], t, sem.at[0]).start()
        _row_copy(ys_ref, d2_ref[0, 0, t], buf.at[1], t, sem.at[1]).start()
        return carry

    lax.fori_loop(0, tm, issue, 0, unroll=ISSUE_UNROLL)
    pltpu.make_async_copy(ys_ref.at[pl.ds(0, tm), :], buf.at[0], sem.at[0]).wait()
    pltpu.make_async_copy(ys_ref.at[pl.ds(0, tm), :], buf.at[1], sem.at[1]).wait()

    gate1 = route_ref[:, ROUTE_G1:ROUTE_G1 + 1]
    gate2 = route_ref[:, ROUTE_G2:ROUTE_G2 + 1]
    moe = gate1 * buf[0] + gate2 * buf[1]
    o_ref[...] = _layer_norm(base_ref[...] + moe, g2_ref[...], b2_ref[...])


def _combine(dest1, dest2, route, base, ys, g2, b2):
    T = base.shape[0]
    tm = TM_ROWS
    idx_spec = pl.BlockSpec((1, 1, tm), lambda i: (i, 0, 0), memory_space=pltpu.SMEM)
    return pl.pallas_call(
        _combine_kernel,
        grid=(T // tm,),
        in_specs=[idx_spec, idx_spec,
                  pl.BlockSpec((tm, LANES), lambda i: (i, 0)),
                  pl.BlockSpec((tm, D_MODEL), lambda i: (i, 0)),
                  pl.BlockSpec(memory_space=pl.ANY),
                  pl.BlockSpec((1, D_MODEL), lambda i: (0, 0)),
                  pl.BlockSpec((1, D_MODEL), lambda i: (0, 0))],
        out_specs=pl.BlockSpec((tm, D_MODEL), lambda i: (i, 0)),
        out_shape=jax.ShapeDtypeStruct((T, D_MODEL), F32),
        scratch_shapes=[pltpu.VMEM((2, tm, D_MODEL), F32), pltpu.SemaphoreType.DMA((2,))],
        compiler_params=_params("arbitrary"),
        name="moe_combine",
    )(dest1.reshape(T // tm, 1, tm), dest2.reshape(T // tm, 1, tm), route, base, ys, g2, b2)


def _pad_lanes(a):
    return jnp.pad(a, ((0, 0), (0, LANES - a.shape[1])))


def _layer(layer, x2, p2, B, S, w_in, b_forget, pool_w, pool_scale, dn_conv, dn_a_log, dn_dt_bias, dn_norm_w,
           w_br_attn, w_br_pool, w_br_dn, w_out, ln1_g, ln1_b, w_rg, b_rg, w_re, b_re,
           w_exp_gate, w_exp_up, w_exp_down, w_pp, w_pg, ln2_g, ln2_b):
    T = B * S
    o = 0
    cols = {}
    for name, width in (("attn", 3 * ATTN_WIDTH), ("forget", ATTN_HEADS), ("pool", POOL_WIDTH),
                        ("dn", 3 * DN_WIDTH), ("decay", DN_HEADS), ("beta", DN_HEADS),
                        ("dgate", DN_WIDTH), ("merge", 3 * D_MODEL)):
        cols[name] = w_in[:, o:o + width]
        o += width
    w_small = _pad_lanes(jnp.concatenate([cols["forget"], cols["decay"], cols["beta"]], axis=1))
    q_scale = jnp.concatenate([jnp.full((ATTN_WIDTH,), LOG2_E * ATTN_HEAD_DIM ** -0.5, F32),
                               jnp.ones((2 * ATTN_WIDTH,), F32)])
    qkv, pu, dqkv, dg, small = _inproj(x2, _mx(cols["attn"] * q_scale), _mx(cols["pool"]), _mx(cols["dn"]),
                                       _mx(cols["dgate"]), _mx(w_small))

    bias_row = _pad_lanes(jnp.concatenate([b_forget, dn_dt_bias])[None, :])
    alog_row = _pad_lanes(jnp.concatenate([jnp.zeros((ATTN_HEADS,), F32), dn_a_log])[None, :])
    gates, gates_t, q_aug, k_aug, v_aug = _gates(small.reshape(B, S, LANES), bias_row, alog_row,
                                              qkv.reshape(B, S, 3 * ATTN_WIDTH))

    y_attn = _fox(q_aug, k_aug, v_aug)
    y_pool = _pool(pu.reshape(B, S, POOL_WIDTH), _mx(pool_w), pool_scale[None, :])
    y_dn = _deltanet(dqkv.reshape(B, S, 3 * DN_WIDTH), dg.reshape(B, S, DN_WIDTH), gates, gates_t,
                     dn_conv, dn_norm_w[None, :])

    w_router = _pad_lanes(jnp.concatenate([w_rg, w_re], axis=1))
    b_router = _pad_lanes(jnp.concatenate([b_rg, b_re])[None, :])
    x1, base, route, counts = _merge(
        x2, y_attn.reshape(T, ATTN_WIDTH), y_pool.reshape(T, POOL_WIDTH), y_dn.reshape(T, DN_WIDTH), p2,
        _mx(cols["merge"]), _mx(w_br_attn), _mx(w_br_pool), _mx(w_br_dn), _mx(w_out),
        ln1_g[None, :], ln1_b[None, :], _mx(w_router), b_router, _mx(w_pg), _mx(w_pp))

    dest1, dest2, pend = _dest(route, counts)
    blk = MOE_BLOCK
    n_rows = 2 * T + N_EXPERTS * blk
    padded_end = pend[0, :N_EXPERTS].astype(jnp.int32)
    block_start = jnp.arange(n_rows // blk, dtype=jnp.int32) * blk
    block_expert = jnp.minimum(jnp.sum(block_start[:, None] >= padded_end[None, :], axis=1),
                               N_EXPERTS - 1).astype(jnp.int32)
    n_real = (padded_end[-1:] // blk).astype(jnp.int32)

    xs = _dispatch(x1, dest1, dest2, padded_end, n_rows)
    ys = _experts(xs, block_expert, n_real, layer, w_exp_gate, w_exp_up, w_exp_down)
    return _combine(dest1, dest2, route, base, ys, ln2_g[None, :], ln2_b[None, :])


def kernel(x, p, w_in, b_forget, pool_w, pool_scale, dn_conv, dn_a_log, dn_dt_bias, dn_norm_w, w_br_attn, w_br_pool, w_br_dn, w_out, ln1_g, ln1_b, w_router_group, b_router_group, w_router_expert, b_router_expert, w_exp_gate, w_exp_up, w_exp_down, w_ple_proj, w_ple_gate, ln2_g, ln2_b):
    B, S, _ = x.shape
    x2 = x.reshape(B * S, D_MODEL)
    for i in range(DEPTH):
        x2 = _layer(i, x2, p[i].reshape(B * S, PLE_DIM), B, S, w_in[i], b_forget[i], pool_w[i], pool_scale[i],
                    dn_conv[i], dn_a_log[i], dn_dt_bias[i], dn_norm_w[i], w_br_attn[i], w_br_pool[i],
                    w_br_dn[i], w_out[i], ln1_g[i], ln1_b[i], w_router_group[i], b_router_group[i],
                    w_router_expert[i], b_router_expert[i], w_exp_gate, w_exp_up, w_exp_down,
                    w_ple_proj[i], w_ple_gate[i], ln2_g[i], ln2_b[i])
    return x2.reshape(B, S, D_MODEL)
```

```python
import functools

import numpy as np
import jax
import jax.numpy as jnp
from jax import lax
from jax.experimental import pallas as pl
from jax.experimental.pallas import tpu as pltpu

F32 = jnp.float32
MXU_DTYPE = jnp.bfloat16

D_MODEL = 1024
DEPTH = 4
PLE_DIM = 256
ATTN_HEADS = 8
ATTN_HEAD_DIM = 64
ATTN_WIDTH = ATTN_HEADS * ATTN_HEAD_DIM
POOL_WINDOWS = (2, 4, 8, 16)
POOL_GROUP_DIM = 128
POOL_WIDTH = len(POOL_WINDOWS) * POOL_GROUP_DIM
DN_HEADS = 4
DN_HEAD_DIM = 128
DN_WIDTH = DN_HEADS * DN_HEAD_DIM
DN_CONV = 4
N_GROUPS = 4
EXPERTS_PER_GROUP = 8
N_EXPERTS = N_GROUPS * EXPERTS_PER_GROUP
D_EXPERT = 512
DEEPNORM_ALPHA = (2 * DEPTH) ** 0.25
LN_EPS = 1e-5
NORM_EPS = 1e-6
NEG_INF = -1e30
LOG2_E = 1.4426950408889634

LANES = 128
SUBLANES = 8
VMEM_LIMIT = 56 * 1024 * 1024

FORGET_LANE = 0
DECAY_LANE = ATTN_HEADS
BETA_LANE = ATTN_HEADS + DN_HEADS
GATE_ROWS = 16

TM_PROJ = 512
TQ_ATTN = 512
TS_POOL = 512
DN_CHUNK = 128
TM_MERGE = 512
TM_ROWS = 1024
TM_DISPATCH = 2048
TM_DEST = 512
MOE_BLOCK = 256
CUM_ROWS = 256


def _dot(a, b):
    return jnp.dot(a, b, preferred_element_type=F32)


def _dot_nt(a, b):
    return lax.dot_general(a, b, (((1,), (1,)), ((), ())), preferred_element_type=F32)


def _dot_exact(a, b):
    return jnp.dot(a, b, preferred_element_type=F32, precision=lax.Precision.HIGHEST)


def _mx(a):
    return a.astype(MXU_DTYPE)


def _sigmoid(z):
    return 1.0 / (1.0 + jnp.exp(-z))


def _silu(z):
    return z * _sigmoid(z)


def _layer_norm(h, g, b):
    mu = jnp.mean(h, axis=-1, keepdims=True)
    hc = h - mu
    var = jnp.mean(hc * hc, axis=-1, keepdims=True)
    return hc * lax.rsqrt(var + LN_EPS) * g + b


def _const_spec(shape):
    zeros = (0,) * len(shape)
    return pl.BlockSpec(shape, lambda *_: zeros, pipeline_mode=pl.Buffered(1))


def _params(*sem):
    return pltpu.CompilerParams(dimension_semantics=sem, vmem_limit_bytes=VMEM_LIMIT)


def _inproj_kernel(x_ref, wa_ref, wp_ref, wd_ref, wg_ref, ws_ref,
                   qkv_ref, pu_ref, dqkv_ref, dg_ref, small_ref):
    xb = _mx(x_ref[...])
    qkv_ref[...] = _dot(xb, wa_ref[...]).astype(qkv_ref.dtype)
    pu_ref[...] = _dot(xb, wp_ref[...]).astype(pu_ref.dtype)
    dqkv_ref[...] = _dot(xb, wd_ref[...]).astype(dqkv_ref.dtype)
    dg_ref[...] = _dot(xb, wg_ref[...]).astype(dg_ref.dtype)
    small_ref[...] = _dot(xb, ws_ref[...])


def _inproj(x2, wa, wp, wd, wg, ws):
    T = x2.shape[0]
    tm = TM_PROJ
    row = lambda n: pl.BlockSpec((tm, n), lambda i: (i, 0))
    return pl.pallas_call(
        _inproj_kernel,
        grid=(T // tm,),
        in_specs=[row(D_MODEL), _const_spec(wa.shape), _const_spec(wp.shape), _const_spec(wd.shape),
                  _const_spec(wg.shape), _const_spec(ws.shape)],
        out_specs=[row(3 * ATTN_WIDTH), row(POOL_WIDTH), row(3 * DN_WIDTH), row(DN_WIDTH), row(LANES)],
        out_shape=[jax.ShapeDtypeStruct((T, 3 * ATTN_WIDTH), MXU_DTYPE),
                   jax.ShapeDtypeStruct((T, POOL_WIDTH), MXU_DTYPE),
                   jax.ShapeDtypeStruct((T, 3 * DN_WIDTH), MXU_DTYPE),
                   jax.ShapeDtypeStruct((T, DN_WIDTH), MXU_DTYPE),
                   jax.ShapeDtypeStruct((T, LANES), F32)],
        compiler_params=_params("parallel"),
        name="inproj",
    )(x2, wa, wp, wd, wg, ws)


BIAS_SPLIT_LANES = 3 * ATTN_HEADS
BIAS_ONES_LANE = BIAS_SPLIT_LANES


def _sum_lane(half):
    return (1 - half) * ATTN_HEAD_DIM


def _bias_placement():
    pq = np.zeros((LANES, ATTN_HEADS * LANES), np.float32)
    pk = np.zeros((LANES, ATTN_HEADS * LANES), np.float32)
    for h in range(ATTN_HEADS):
        base = h * LANES + (1 - h % 2) * ATTN_HEAD_DIM
        for part in range(3):
            pq[part * ATTN_HEADS + h, base + part] = 1.0
            pq[BIAS_ONES_LANE, base + 3 + part] = 1.0
            pk[BIAS_ONES_LANE, base + part] = 1.0
            pk[part * ATTN_HEADS + h, base + 3 + part] = -1.0
    return pq, pk


def _gates_kernel(small_ref, bias_ref, alog_ref, qkv_ref, pq_ref, pk_ref,
                  g_ref, gt_ref, qa_ref, ka_ref, va_ref, carry_sc):
    R = small_ref.shape[1]

    @pl.when(pl.program_id(1) == 0)
    def _():
        carry_sc[...] = jnp.zeros(carry_sc.shape, F32)

    ri = lax.broadcasted_iota(jnp.int32, (R, R), 0)
    ci = lax.broadcasted_iota(jnp.int32, (R, R), 1)
    tri_full = (ri >= ci).astype(F32)
    tri_chunk = ((ri >= ci) & (ri // DN_CHUNK == ci // DN_CHUNK)).astype(F32)
    lane = lax.broadcasted_iota(jnp.int32, (R, LANES), 1)
    is_forget = lane < DECAY_LANE
    is_decay = (lane >= DECAY_LANE) & (lane < BETA_LANE)

    z = small_ref[0] + bias_ref[...]
    soft = jnp.log(1.0 + jnp.exp(-jnp.abs(z)))
    log_f = jnp.minimum(z, 0.0) - soft
    decay = -jnp.exp(alog_ref[...]) * (jnp.maximum(z, 0.0) + soft)
    vals = jnp.where(is_forget, log_f, jnp.where(is_decay, decay, _sigmoid(z)))
    cum_full = _dot_exact(tri_full, vals) + carry_sc[...]
    cum_chunk = _dot_exact(tri_chunk, vals)
    carry_sc[...] = cum_full[R - 1:R, :]
    out = jnp.where(is_forget, cum_full, jnp.where(is_decay, cum_chunk, vals))
    g_ref[0] = out
    gt_ref[0] = out.T[:GATE_ROWS, :]

    c = jnp.where(is_forget, out, 0.0) * LOG2_E
    hi = _mx(c).astype(F32)
    mid = _mx(c - hi).astype(F32)
    lo = _mx(c - hi - mid).astype(F32)
    split = (hi + pltpu.roll(mid, ATTN_HEADS, axis=1) + pltpu.roll(lo, 2 * ATTN_HEADS, axis=1)
             + jnp.where(lane == BIAS_ONES_LANE, 1.0, 0.0))
    split_b = _mx(split)
    extra_q = _dot(split_b, pq_ref[...]).astype(qa_ref.dtype)
    extra_k = _dot(split_b, pk_ref[...]).astype(ka_ref.dtype)
    lane1 = lax.broadcasted_iota(jnp.int32, (1, LANES), 1)
    for pair in range(ATTN_HEADS // 2):
        qp = qkv_ref[0, :, pair * LANES:(pair + 1) * LANES]
        kp = qkv_ref[0, :, ATTN_WIDTH + pair * LANES:ATTN_WIDTH + (pair + 1) * LANES]
        vp = qkv_ref[0, :, 2 * ATTN_WIDTH + pair * LANES:2 * ATTN_WIDTH + (pair + 1) * LANES]
        for half in range(2):
            h = 2 * pair + half
            own = (lane1 >= half * ATTN_HEAD_DIM) & (lane1 < (half + 1) * ATTN_HEAD_DIM)
            ones = jnp.where(lane1 == _sum_lane(half), 1.0, 0.0).astype(va_ref.dtype)
            qa_ref[0, h] = jnp.where(own, qp, extra_q[:, h * LANES:(h + 1) * LANES])
            ka_ref[0, h] = jnp.where(own, kp, extra_k[:, h * LANES:(h + 1) * LANES])
            va_ref[0, h] = jnp.where(own, vp, ones)


def _gates(small3, bias_row, alog_row, qkv3):
    B, S, _ = small3.shape
    R = CUM_ROWS
    pq, pk = _bias_placement()
    pq = jnp.asarray(pq, MXU_DTYPE)
    pk = jnp.asarray(pk, MXU_DTYPE)
    H = ATTN_HEADS
    return pl.pallas_call(
        _gates_kernel,
        grid=(B, S // R),
        in_specs=[pl.BlockSpec((1, R, LANES), lambda b, r: (b, r, 0)),
                  pl.BlockSpec((1, LANES), lambda b, r: (0, 0)),
                  pl.BlockSpec((1, LANES), lambda b, r: (0, 0)),
                  pl.BlockSpec((1, R, 3 * ATTN_WIDTH), lambda b, r: (b, r, 0)),
                  pl.BlockSpec(pq.shape, lambda b, r: (0, 0)),
                  pl.BlockSpec(pk.shape, lambda b, r: (0, 0))],
        out_specs=[pl.BlockSpec((1, R, LANES), lambda b, r: (b, r, 0)),
                   pl.BlockSpec((1, GATE_ROWS, R), lambda b, r: (b, 0, r)),
                   pl.BlockSpec((1, H, R, LANES), lambda b, r: (b, 0, r, 0)),
                   pl.BlockSpec((1, H, R, LANES), lambda b, r: (b, 0, r, 0)),
                   pl.BlockSpec((1, H, R, LANES), lambda b, r: (b, 0, r, 0))],
        out_shape=[jax.ShapeDtypeStruct((B, S, LANES), F32),
                   jax.ShapeDtypeStruct((B, GATE_ROWS, S), F32),
                   jax.ShapeDtypeStruct((B, H, S, LANES), MXU_DTYPE),
                   jax.ShapeDtypeStruct((B, H, S, LANES), MXU_DTYPE),
                   jax.ShapeDtypeStruct((B, H, S, LANES), MXU_DTYPE)],
        scratch_shapes=[pltpu.VMEM((1, LANES), F32)],
        compiler_params=_params("parallel", "arbitrary"),
        name="gates",
    )(small3, bias_row, alog_row, qkv3, pq, pk)


def _fox_kernel(qi_ref, kj_ref, q_ref, k_ref, v_ref, o_ref, m_sc, acc_sc):
    step = pl.program_id(1)
    i = qi_ref[step]
    j = kj_ref[step]
    tq = q_ref.shape[2]
    tk = k_ref.shape[2]
    reps = tk // LANES

    @pl.when(j == 0)
    def _():
        m_sc[...] = jnp.full(m_sc.shape, NEG_INF, F32)
        acc_sc[...] = jnp.zeros(acc_sc.shape, F32)

    def sweep(masked):
        if masked:
            rows = lax.broadcasted_iota(jnp.int32, (tq, tk), 0)
            cols = lax.broadcasted_iota(jnp.int32, (tq, tk), 1)
            keep = cols <= rows
        for h in range(ATTN_HEADS):
            s = _dot_nt(q_ref[0, h], k_ref[0, h])
            if masked:
                s = jnp.where(keep, s, NEG_INF)
            m_prev = m_sc[h]
            m_next = jnp.maximum(m_prev, jnp.max(s, axis=1)[:, None])
            p = jnp.exp2(s - jnp.concatenate([m_next] * reps, axis=1))
            m_sc[h] = m_next
            acc_sc[h] = jnp.exp2(m_prev - m_next) * acc_sc[h] + _dot(_mx(p), v_ref[0, h])

    @pl.when(j < i)
    def _():
        sweep(False)

    @pl.when(j == i)
    def _():
        sweep(True)
        lane = lax.broadcasted_iota(jnp.int32, (1, LANES), 1)
        for pair in range(ATTN_HEADS // 2):
            halves = []
            for half in range(2):
                acc = acc_sc[2 * pair + half]
                halves.append(acc / acc[:, _sum_lane(half):_sum_lane(half) + 1])
            out = jnp.where(lane < ATTN_HEAD_DIM, halves[0], halves[1])
            o_ref[0, :, pair * LANES:(pair + 1) * LANES] = out.astype(o_ref.dtype)


def _fox(q_aug, k_aug, v_aug):
    B, H, S, _ = q_aug.shape
    t = TQ_ATTN
    n = S // t
    pairs = [(i, j) for i in range(n) for j in range(i + 1)]
    qi = jnp.asarray([i for i, _ in pairs], jnp.int32)
    kj = jnp.asarray([j for _, j in pairs], jnp.int32)
    grid_spec = pltpu.PrefetchScalarGridSpec(
        num_scalar_prefetch=2,
        grid=(B, len(pairs)),
        in_specs=[pl.BlockSpec((1, H, t, LANES), lambda b, s, qi, kj: (b, 0, qi[s], 0)),
                  pl.BlockSpec((1, H, t, LANES), lambda b, s, qi, kj: (b, 0, kj[s], 0)),
                  pl.BlockSpec((1, H, t, LANES), lambda b, s, qi, kj: (b, 0, kj[s], 0))],
        out_specs=pl.BlockSpec((1, t, ATTN_WIDTH), lambda b, s, qi, kj: (b, qi[s], 0)),
        scratch_shapes=[pltpu.VMEM((ATTN_HEADS, t, LANES), F32),
                        pltpu.VMEM((ATTN_HEADS, t, LANES), F32)],
    )
    return pl.pallas_call(
        _fox_kernel,
        grid_spec=grid_spec,
        out_shape=jax.ShapeDtypeStruct((B, S, ATTN_WIDTH), MXU_DTYPE),
        compiler_params=_params("parallel", "arbitrary"),
        name="fox_attention",
    )(qi, kj, q_aug, k_aug, v_aug)


POOL_HALO = 16


def _pool_kernel(u_ref, w_ref, scale_ref, y_ref, halo_sc):
    s_idx = pl.program_id(1)
    ts = u_ref.shape[1]

    @pl.when(s_idx == 0)
    def _():
        halo_sc[...] = jnp.zeros(halo_sc.shape, F32)

    u = u_ref[0].astype(F32)
    ext = jnp.concatenate([halo_sc[...], u], axis=0)
    halo_sc[...] = u[ts - POOL_HALO:, :]
    pos = (lax.broadcasted_iota(jnp.int32, (ts, 1), 0) + s_idx * ts + 1).astype(F32)
    for g, w in enumerate(POOL_WINDOWS):
        sl = slice(g * POOL_GROUP_DIM, (g + 1) * POOL_GROUP_DIM)
        acc = ext[:, sl]
        span = 1
        while span < w:
            acc = acc + pltpu.roll(acc, span, axis=0)
            span *= 2
        mean = acc[POOL_HALO:, :] / jnp.minimum(pos, float(w))
        d = mean - u[:, sl]
        y = _dot(_mx(d), w_ref[g]) * scale_ref[:, sl]
        y_ref[0, :, sl] = y.astype(y_ref.dtype)


def _pool(pu3, pool_w, pool_scale_row):
    B, S, _ = pu3.shape
    ts = TS_POOL
    return pl.pallas_call(
        _pool_kernel,
        grid=(B, S // ts),
        in_specs=[pl.BlockSpec((1, ts, POOL_WIDTH), lambda b, s: (b, s, 0)),
                  pl.BlockSpec(pool_w.shape, lambda b, s: (0, 0, 0)),
                  pl.BlockSpec((1, POOL_WIDTH), lambda b, s: (0, 0))],
        out_specs=pl.BlockSpec((1, ts, POOL_WIDTH), lambda b, s: (b, s, 0)),
        out_shape=jax.ShapeDtypeStruct((B, S, POOL_WIDTH), MXU_DTYPE),
        scratch_shapes=[pltpu.VMEM((POOL_HALO, POOL_WIDTH), F32)],
        compiler_params=_params("parallel", "arbitrary"),
        name="pool",
    )(pu3, pool_w, pool_scale_row)


DN_HALO = SUBLANES


def _unit_lower_inverses(lows):
    n = lows[0].shape[0]
    eye = (lax.broadcasted_iota(jnp.int32, (n, n), 0) == lax.broadcasted_iota(jnp.int32, (n, n), 1)).astype(F32)
    lows_b = [_mx(low) for low in lows]
    invs = [eye - low for low in lows]
    powers = [_dot(lb, lb) for lb in lows_b]
    span = 2
    while True:
        powers_b = [_mx(pw) for pw in powers]
        invs = [inv + _dot(_mx(inv), pb) for inv, pb in zip(invs, powers_b)]
        span *= 2
        if span >= n:
            return invs
        powers = [_dot(pb, pb) for pb in powers_b]


def _dn_kernel(qkv_ref, dg_ref, g_ref, gt_ref, cw_ref, nw_ref, o_ref, halo_sc, state_sc):
    c_idx = pl.program_id(0)
    NB, C = qkv_ref.shape[0], qkv_ref.shape[1]

    @pl.when(c_idx == 0)
    def _():
        halo_sc[...] = jnp.zeros(halo_sc.shape, F32)
        state_sc[...] = jnp.zeros(state_sc.shape, F32)

    ri = lax.broadcasted_iota(jnp.int32, (C, C), 0)
    ci = lax.broadcasted_iota(jnp.int32, (C, C), 1)
    causal = ri >= ci
    strict = ri > ci

    chains = [(b, h) for b in range(NB) for h in range(DN_HEADS)]
    qs, ks, vs = [], [], []
    for b in range(NB):
        u = qkv_ref[b].astype(F32)
        ext = jnp.concatenate([halo_sc[b], u], axis=0)
        halo_sc[b] = u[C - DN_HALO:, :]
        conv = u * cw_ref[DN_CONV - 1:DN_CONV, :]
        for d in range(1, DN_CONV):
            conv = conv + pltpu.roll(ext, d, axis=0)[DN_HALO:, :] * cw_ref[DN_CONV - 1 - d:DN_CONV - d, :]
        act = _silu(conv)
        for h in range(DN_HEADS):
            q = act[:, h * DN_HEAD_DIM:(h + 1) * DN_HEAD_DIM]
            k = act[:, DN_WIDTH + h * DN_HEAD_DIM:DN_WIDTH + (h + 1) * DN_HEAD_DIM]
            qs.append(q * lax.rsqrt(jnp.sum(q * q, axis=-1, keepdims=True) + NORM_EPS) * (DN_HEAD_DIM ** -0.5))
            ks.append(k * lax.rsqrt(jnp.sum(k * k, axis=-1, keepdims=True) + NORM_EPS))
            vs.append(act[:, 2 * DN_WIDTH + h * DN_HEAD_DIM:2 * DN_WIDTH + (h + 1) * DN_HEAD_DIM])

    gcs = [g_ref[b, :, DECAY_LANE + h:DECAY_LANE + h + 1] for b, h in chains]
    betas = [g_ref[b, :, BETA_LANE + h:BETA_LANE + h + 1] for b, h in chains]
    gc_rows = [gt_ref[b, DECAY_LANE + h:DECAY_LANE + h + 1, :] for b, h in chains]
    g_lasts = [gc[C - 1:C, :] for gc in gcs]
    decays = [jnp.where(causal, jnp.exp(jnp.where(causal, gc - gr, 0.0)), 0.0) for gc, gr in zip(gcs, gc_rows)]
    kbs = [_mx(k) for k in ks]
    kks = [_dot_nt(kb, kb) for kb in kbs]
    qks = [_dot_nt(_mx(q), kb) for q, kb in zip(qs, kbs)]
    lows = [jnp.where(strict, beta * kk * decay, 0.0) for beta, kk, decay in zip(betas, kks, decays)]
    attns = [_mx(jnp.where(causal, qk * decay, 0.0)) for qk, decay in zip(qks, decays)]
    invs = _unit_lower_inverses(lows)
    e_gcs = [jnp.exp(gc) for gc in gcs]
    rhss = [_mx(jnp.concatenate([v * beta, k * (beta * eg)], axis=1)) for v, k, beta, eg in zip(vs, ks, betas, e_gcs)]
    sols = [_dot(_mx(inv), rhs) for inv, rhs in zip(invs, rhss)]
    q_decs = [_mx(q * eg) for q, eg in zip(qs, e_gcs)]
    k_dec_ts = [_mx((k * jnp.exp(gl - gc)).T) for k, gl, gc in zip(ks, g_lasts, gcs)]

    states = [state_sc[b, h] for b, h in chains]
    states_b = [_mx(st) for st in states]
    u_news = [_mx(sol[:, :DN_HEAD_DIM] - _dot(_mx(sol[:, DN_HEAD_DIM:]), sb)) for sol, sb in zip(sols, states_b)]
    outs = [_dot(qd, sb) + _dot(at, un) for qd, sb, at, un in zip(q_decs, states_b, attns, u_news)]
    for (b, h), st, gl, kt, un in zip(chains, states, g_lasts, k_dec_ts, u_news):
        state_sc[b, h] = st * jnp.exp(gl) + _dot(kt, un)
    for (b, h), o in zip(chains, outs):
        sl = slice(h * DN_HEAD_DIM, (h + 1) * DN_HEAD_DIM)
        o = o * lax.rsqrt(jnp.mean(o * o, axis=-1, keepdims=True) + NORM_EPS) * nw_ref[...]
        o = o * _silu(dg_ref[b, :, sl].astype(F32))
        o_ref[b, :, sl] = o.astype(o_ref.dtype)


def _deltanet(dqkv3, dg3, gates, gates_t, conv_w, norm_w_row):
    B, S, _ = dqkv3.shape
    C = DN_CHUNK
    return pl.pallas_call(
        _dn_kernel,
        grid=(S // C,),
        in_specs=[pl.BlockSpec((B, C, 3 * DN_WIDTH), lambda c: (0, c, 0)),
                  pl.BlockSpec((B, C, DN_WIDTH), lambda c: (0, c, 0)),
                  pl.BlockSpec((B, C, LANES), lambda c: (0, c, 0)),
                  pl.BlockSpec((B, GATE_ROWS, C), lambda c: (0, 0, c)),
                  pl.BlockSpec((DN_CONV, 3 * DN_WIDTH), lambda c: (0, 0)),
                  pl.BlockSpec((1, DN_HEAD_DIM), lambda c: (0, 0))],
        out_specs=pl.BlockSpec((B, C, DN_WIDTH), lambda c: (0, c, 0)),
        out_shape=jax.ShapeDtypeStruct((B, S, DN_WIDTH), MXU_DTYPE),
        scratch_shapes=[pltpu.VMEM((B, DN_HALO, 3 * DN_WIDTH), F32),
                        pltpu.VMEM((B, DN_HEADS, DN_HEAD_DIM, DN_HEAD_DIM), F32)],
        compiler_params=_params("arbitrary"),
        name="deltanet",
    )(dqkv3, dg3, gates, gates_t, conv_w, norm_w_row)


ROUTE_E1, ROUTE_E2, ROUTE_G1, ROUTE_G2 = range(4)
ROUTER_EXPERT_LANE = N_GROUPS


def _merge_kernel(x_ref, ya_ref, yp_ref, yd_ref, p_ref, wgate_ref, wa_ref, wp_ref, wd_ref, wo_ref,
                  g1_ref, b1_ref, wr_ref, br_ref, wpg_ref, wpp_ref,
                  x1_ref, base_ref, route_ref, cnt_ref, cnt_sc):
    tm = x_ref.shape[0]

    @pl.when(pl.program_id(0) == 0)
    def _():
        cnt_sc[...] = jnp.zeros(cnt_sc.shape, F32)

    x = x_ref[...]
    gate = _dot(_mx(x), wgate_ref[...])
    merged = (_sigmoid(gate[:, :D_MODEL]) * _dot(ya_ref[...], wa_ref[...])
              + _sigmoid(gate[:, D_MODEL:2 * D_MODEL]) * _dot(yp_ref[...], wp_ref[...])
              + _sigmoid(gate[:, 2 * D_MODEL:]) * _dot(yd_ref[...], wd_ref[...]))
    mix = _dot(_mx(merged), wo_ref[...])
    x1 = _layer_norm(DEEPNORM_ALPHA * x + mix, g1_ref[...], b1_ref[...])
    x1_ref[...] = x1
    x1b = _mx(x1)
    ple = _sigmoid(_dot(x1b, wpg_ref[...])) * _dot(_mx(p_ref[...]), wpp_ref[...])
    base_ref[...] = DEEPNORM_ALPHA * x1 + ple

    logits = _dot(x1b, wr_ref[...]) + br_ref[...]
    lane = lax.broadcasted_iota(jnp.int32, (tm, LANES), 1)
    ninf = -jnp.inf

    def first_argmax(vals):
        top = jnp.max(vals, axis=-1, keepdims=True)
        idx = jnp.min(jnp.where(vals == top, lane, LANES), axis=-1, keepdims=True)
        return top, idx

    group_logits = jnp.where(lane < N_GROUPS, logits, ninf)
    g_top, g_idx = first_argmax(group_logits)
    p_group = 1.0 / jnp.sum(jnp.exp(group_logits - g_top), axis=-1, keepdims=True)
    lo = ROUTER_EXPERT_LANE + EXPERTS_PER_GROUP * g_idx
    local = jnp.where((lane >= lo) & (lane < lo + EXPERTS_PER_GROUP), logits, ninf)
    top1, idx1 = first_argmax(local)
    top2, idx2 = first_argmax(jnp.where(lane == idx1, ninf, local))
    z = jnp.sum(jnp.exp(local - top1), axis=-1, keepdims=True)
    prob1 = 1.0 / z
    prob2 = jnp.exp(top2 - top1) / z
    gate1 = p_group * prob1 / (prob1 + prob2)
    gate2 = p_group * prob2 / (prob1 + prob2)
    e1 = idx1 - ROUTER_EXPERT_LANE
    e2 = idx2 - ROUTER_EXPERT_LANE
    cnt_sc[...] = cnt_sc[...] + jnp.sum(((lane == e1) | (lane == e2)).astype(F32), axis=0, keepdims=True)
    cnt_ref[...] = jnp.broadcast_to(cnt_sc[...], cnt_ref.shape)
    route = jnp.zeros((tm, LANES), F32)
    for pos, val in ((ROUTE_E1, e1.astype(F32)), (ROUTE_E2, e2.astype(F32)),
                     (ROUTE_G1, gate1), (ROUTE_G2, gate2)):
        route = jnp.where(lane == pos, val, route)
    route_ref[...] = route


def _merge(x2, ya, yp, yd, p2, wgate, wa, wp, wd, wo, g1, b1, wr, br, wpg, wpp):
    T = x2.shape[0]
    tm = TM_MERGE
    row = lambda n: pl.BlockSpec((tm, n), lambda i: (i, 0))
    consts = [wgate, wa, wp, wd, wo, g1, b1, wr, br, wpg, wpp]
    return pl.pallas_call(
        _merge_kernel,
        grid=(T // tm,),
        in_specs=[row(D_MODEL), row(ATTN_WIDTH), row(POOL_WIDTH), row(DN_WIDTH), row(PLE_DIM)]
                 + [_const_spec(c.shape) for c in consts],
        out_specs=[row(D_MODEL), row(D_MODEL), row(LANES), pl.BlockSpec((SUBLANES, LANES), lambda i: (0, 0))],
        out_shape=[jax.ShapeDtypeStruct((T, D_MODEL), F32),
                   jax.ShapeDtypeStruct((T, D_MODEL), F32),
                   jax.ShapeDtypeStruct((T, LANES), F32),
                   jax.ShapeDtypeStruct((SUBLANES, LANES), F32)],
        scratch_shapes=[pltpu.VMEM((1, LANES), F32)],
        compiler_params=_params("arbitrary"),
        name="merge_router",
    )(x2, ya, yp, yd, p2, *consts)


def _dest_kernel(route_ref, cnt_ref, d1_ref, d2_ref, pend_ref, start_sc):
    tm = route_ref.shape[0]
    lane = lax.broadcasted_iota(jnp.int32, (tm, LANES), 1)
    e1 = route_ref[:, ROUTE_E1:ROUTE_E1 + 1].astype(jnp.int32)
    e2 = route_ref[:, ROUTE_E2:ROUTE_E2 + 1].astype(jnp.int32)
    onehot = ((lane == e1) | (lane == e2)).astype(F32)

    @pl.when(pl.program_id(0) == 0)
    def _():
        blk = float(MOE_BLOCK)
        padded = jnp.floor((cnt_ref[...] + (blk - 1.0)) / blk) * blk
        ri = lax.broadcasted_iota(jnp.int32, (LANES, LANES), 0)
        ci = lax.broadcasted_iota(jnp.int32, (LANES, LANES), 1)
        ends = _dot_exact(padded, (ri <= ci).astype(F32))
        start_sc[...] = ends[:1] - padded[:1]
        pend_ref[...] = ends

    ri = lax.broadcasted_iota(jnp.int32, (tm, tm), 0)
    ci = lax.broadcasted_iota(jnp.int32, (tm, tm), 1)
    before = _dot((ri > ci).astype(jnp.bfloat16), onehot.astype(jnp.bfloat16)) + start_sc[...]
    dest = jnp.where(lane == 0, jnp.sum(jnp.where(lane == e1, before, 0.0), axis=-1, keepdims=True),
                     jnp.where(lane == 1, jnp.sum(jnp.where(lane == e2, before, 0.0), axis=-1, keepdims=True), 0.0))
    pick = (lax.broadcasted_iota(jnp.int32, (SUBLANES, LANES), 1)
            == lax.broadcasted_iota(jnp.int32, (SUBLANES, LANES), 0)).astype(F32)
    rows = lax.dot_general(pick, dest, (((1,), (1,)), ((), ())), preferred_element_type=F32,
                           precision=lax.Precision.HIGHEST)
    d1_ref[...] = rows[0:1, :].astype(jnp.int32)
    d2_ref[...] = rows[1:2, :].astype(jnp.int32)
    start_sc[...] = start_sc[...] + jnp.sum(onehot, axis=0, keepdims=True)


def _dest(route, counts):
    T = route.shape[0]
    tm = TM_DEST
    return pl.pallas_call(
        _dest_kernel,
        grid=(T // tm,),
        in_specs=[pl.BlockSpec((tm, LANES), lambda i: (i, 0)),
                  pl.BlockSpec((SUBLANES, LANES), lambda i: (0, 0))],
        out_specs=[pl.BlockSpec((1, tm), lambda i: (0, i)),
                   pl.BlockSpec((1, tm), lambda i: (0, i)),
                   pl.BlockSpec((SUBLANES, LANES), lambda i: (0, 0))],
        out_shape=[jax.ShapeDtypeStruct((1, T), jnp.int32),
                   jax.ShapeDtypeStruct((1, T), jnp.int32),
                   jax.ShapeDtypeStruct((SUBLANES, LANES), F32)],
        scratch_shapes=[pltpu.VMEM((1, LANES), F32)],
        compiler_params=_params("arbitrary"),
        name="moe_dest",
    )(route, counts)


ISSUE_UNROLL = 8


def _row_copy(src, src_row, dst, dst_row, sem):
    return pltpu.make_async_copy(src.at[pl.ds(src_row, 1), :], dst.at[pl.ds(dst_row, 1), :], sem)


def _dispatch_kernel(pend_ref, d1_ref, d2_ref, x_ref, xs_ref, zero_sc, sem, zsem):
    tm = x_ref.shape[0]
    n_blocks = xs_ref.shape[0] // MOE_BLOCK
    used_blocks = pend_ref[N_EXPERTS - 1] // MOE_BLOCK

    def fill(block):
        first = pl.multiple_of(block * MOE_BLOCK, MOE_BLOCK)
        return pltpu.make_async_copy(zero_sc, xs_ref.at[pl.ds(first, MOE_BLOCK), :], zsem)

    def fills():
        for e in range(N_EXPERTS):
            yield pend_ref[e] // MOE_BLOCK - 1, pend_ref[e] >= MOE_BLOCK
        for t in range(N_EXPERTS):
            yield used_blocks + t, used_blocks + t < n_blocks

    @pl.when(pl.program_id(0) == 0)
    def _():
        zero_sc[...] = jnp.zeros(zero_sc.shape, F32)
        for block, exists in fills():
            @pl.when(exists)
            def _():
                fill(block).start()
        for block, exists in fills():
            @pl.when(exists)
            def _():
                fill(block).wait()

    def issue(t, carry):
        _row_copy(x_ref, t, xs_ref, d1_ref[0, 0, t], sem.at[0]).start()
        _row_copy(x_ref, t, xs_ref, d2_ref[0, 0, t], sem.at[1]).start()
        return carry

    lax.fori_loop(0, tm, issue, 0, unroll=ISSUE_UNROLL)
    pltpu.make_async_copy(x_ref, xs_ref.at[pl.ds(0, tm), :], sem.at[0]).wait()
    pltpu.make_async_copy(x_ref, xs_ref.at[pl.ds(0, tm), :], sem.at[1]).wait()


def _dispatch(x1, dest1, dest2, padded_end, n_rows):
    T = x1.shape[0]
    tm = TM_DISPATCH
    idx_spec = pl.BlockSpec((1, 1, tm), lambda i, pe: (i, 0, 0), memory_space=pltpu.SMEM)
    grid_spec = pltpu.PrefetchScalarGridSpec(
        num_scalar_prefetch=1,
        grid=(T // tm,),
        in_specs=[idx_spec, idx_spec, pl.BlockSpec((tm, D_MODEL), lambda i, pe: (i, 0))],
        out_specs=pl.BlockSpec(memory_space=pl.ANY),
        scratch_shapes=[pltpu.VMEM((MOE_BLOCK, D_MODEL), F32), pltpu.SemaphoreType.DMA((2,)),
                        pltpu.SemaphoreType.DMA(())],
    )
    return pl.pallas_call(
        _dispatch_kernel,
        grid_spec=grid_spec,
        out_shape=jax.ShapeDtypeStruct((n_rows, D_MODEL), F32),
        compiler_params=_params("arbitrary"),
        name="moe_dispatch",
    )(padded_end, dest1.reshape(T // tm, 1, tm), dest2.reshape(T // tm, 1, tm), x1)


def _experts_kernel(bexp_ref, nreal_ref, xs_ref, wg_ref, wu_ref, wd_ref, ys_ref, wg_sc, wu_sc, wd_sc):
    i = pl.program_id(0)
    changed = (i == 0) | (bexp_ref[i] != bexp_ref[jnp.maximum(i - 1, 0)])

    @pl.when(changed)
    def _():
        wg_sc[...] = _mx(wg_ref[0, 0])
        wu_sc[...] = _mx(wu_ref[0, 0])
        wd_sc[...] = _mx(wd_ref[0, 0])

    @pl.when(i < nreal_ref[0])
    def _():
        xb = _mx(xs_ref[...])
        hid = _silu(_dot(xb, wg_sc[...])) * _dot(xb, wu_sc[...])
        ys_ref[...] = _dot(_mx(hid), wd_sc[...])

    @pl.when(i >= nreal_ref[0])
    def _():
        ys_ref[...] = jnp.zeros(ys_ref.shape, F32)


def _experts(xs, block_expert, n_real, layer, w_gate, w_up, w_down):
    n_rows = xs.shape[0]
    blk = MOE_BLOCK
    grid_spec = pltpu.PrefetchScalarGridSpec(
        num_scalar_prefetch=2,
        grid=(n_rows // blk,),
        in_specs=[pl.BlockSpec((blk, D_MODEL), lambda i, be, nr: (jnp.minimum(i, nr[0] - 1), 0)),
                  pl.BlockSpec((1, 1, D_MODEL, D_EXPERT), lambda i, be, nr: (layer, be[i], 0, 0)),
                  pl.BlockSpec((1, 1, D_MODEL, D_EXPERT), lambda i, be, nr: (layer, be[i], 0, 0)),
                  pl.BlockSpec((1, 1, D_EXPERT, D_MODEL), lambda i, be, nr: (layer, be[i], 0, 0))],
        out_specs=pl.BlockSpec((blk, D_MODEL), lambda i, be, nr: (i, 0)),
        scratch_shapes=[pltpu.VMEM((D_MODEL, D_EXPERT), MXU_DTYPE),
                        pltpu.VMEM((D_MODEL, D_EXPERT), MXU_DTYPE),
                        pltpu.VMEM((D_EXPERT, D_MODEL), MXU_DTYPE)],
    )
    return pl.pallas_call(
        _experts_kernel,
        grid_spec=grid_spec,
        out_shape=jax.ShapeDtypeStruct((n_rows, D_MODEL), F32),
        compiler_params=_params("arbitrary"),
        name="moe_experts",
    )(block_expert, n_real, xs, w_gate, w_up, w_down)


def _combine_kernel(d1_ref, d2_ref, route_ref, base_ref, ys_ref, g2_ref, b2_ref, o_ref, buf, sem):
    tm = base_ref.shape[0]

    def issue(t, carry):
        _row_copy(ys_ref, d1_ref[0, 0, t], buf.at[0], t, sem.at[0]).start()
        _row_copy(ys_ref, d2_ref[0, 0, t], buf.at[1], t, sem.at[1]).start()
        return carry

    lax.fori_loop(0, tm, issue, 0, unroll=ISSUE_UNROLL)
    pltpu.make_async_copy(ys_ref.at[pl.ds(0, tm), :], buf.at[0], sem.at[0]).wait()
    pltpu.make_async_copy(ys_ref.at[pl.ds(0, tm), :], buf.at[1], sem.at[1]).wait()

    gate1 = route_ref[:, ROUTE_G1:ROUTE_G1 + 1]
    gate2 = route_ref[:, ROUTE_G2:ROUTE_G2 + 1]
    moe = gate1 * buf[0] + gate2 * buf[1]
    o_ref[...] = _layer_norm(base_ref[...] + moe, g2_ref[...], b2_ref[...])


def _combine(dest1, dest2, route, base, ys, g2, b2):
    T = base.shape[0]
    tm = TM_ROWS
    idx_spec = pl.BlockSpec((1, 1, tm), lambda i: (i, 0, 0), memory_space=pltpu.SMEM)
    return pl.pallas_call(
        _combine_kernel,
        grid=(T // tm,),
        in_specs=[idx_spec, idx_spec,
                  pl.BlockSpec((tm, LANES), lambda i: (i, 0)),
                  pl.BlockSpec((tm, D_MODEL), lambda i: (i, 0)),
                  pl.BlockSpec(memory_space=pl.ANY),
                  pl.BlockSpec((1, D_MODEL), lambda i: (0, 0)),
                  pl.BlockSpec((1, D_MODEL), lambda i: (0, 0))],
        out_specs=pl.BlockSpec((tm, D_MODEL), lambda i: (i, 0)),
        out_shape=jax.ShapeDtypeStruct((T, D_MODEL), F32),
        scratch_shapes=[pltpu.VMEM((2, tm, D_MODEL), F32), pltpu.SemaphoreType.DMA((2,))],
        compiler_params=_params("arbitrary"),
        name="moe_combine",
    )(dest1.reshape(T // tm, 1, tm), dest2.reshape(T // tm, 1, tm), route, base, ys, g2, b2)


def _pad_lanes(a):
    return jnp.pad(a, ((0, 0), (0, LANES - a.shape[1])))


def _layer(layer, x2, p2, B, S, w_in, b_forget, pool_w, pool_scale, dn_conv, dn_a_log, dn_dt_bias, dn_norm_w,
           w_br_attn, w_br_pool, w_br_dn, w_out, ln1_g, ln1_b, w_rg, b_rg, w_re, b_re,
           w_exp_gate, w_exp_up, w_exp_down, w_pp, w_pg, ln2_g, ln2_b):
    T = B * S
    o = 0
    cols = {}
    for name, width in (("attn", 3 * ATTN_WIDTH), ("forget", ATTN_HEADS), ("pool", POOL_WIDTH),
                        ("dn", 3 * DN_WIDTH), ("decay", DN_HEADS), ("beta", DN_HEADS),
                        ("dgate", DN_WIDTH), ("merge", 3 * D_MODEL)):
        cols[name] = w_in[:, o:o + width]
        o += width
    w_small = _pad_lanes(jnp.concatenate([cols["forget"], cols["decay"], cols["beta"]], axis=1))
    q_scale = jnp.concatenate([jnp.full((ATTN_WIDTH,), LOG2_E * ATTN_HEAD_DIM ** -0.5, F32),
                               jnp.ones((2 * ATTN_WIDTH,), F32)])
    qkv, pu, dqkv, dg, small = _inproj(x2, _mx(cols["attn"] * q_scale), _mx(cols["pool"]), _mx(cols["dn"]),
                                       _mx(cols["dgate"]), _mx(w_small))

    bias_row = _pad_lanes(jnp.concatenate([b_forget, dn_dt_bias])[None, :])
    alog_row = _pad_lanes(jnp.concatenate([jnp.zeros((ATTN_HEADS,), F32), dn_a_log])[None, :])
    gates, gates_t, q_aug, k_aug, v_aug = _gates(small.reshape(B, S, LANES), bias_row, alog_row,
                                              qkv.reshape(B, S, 3 * ATTN_WIDTH))

    y_attn = _fox(q_aug, k_aug, v_aug)
    y_pool = _pool(pu.reshape(B, S, POOL_WIDTH), _mx(pool_w), pool_scale[None, :])
    y_dn = _deltanet(dqkv.reshape(B, S, 3 * DN_WIDTH), dg.reshape(B, S, DN_WIDTH), gates, gates_t,
                     dn_conv, dn_norm_w[None, :])

    w_router = _pad_lanes(jnp.concatenate([w_rg, w_re], axis=1))
    b_router = _pad_lanes(jnp.concatenate([b_rg, b_re])[None, :])
    x1, base, route, counts = _merge(
        x2, y_attn.reshape(T, ATTN_WIDTH), y_pool.reshape(T, POOL_WIDTH), y_dn.reshape(T, DN_WIDTH), p2,
        _mx(cols["merge"]), _mx(w_br_attn), _mx(w_br_pool), _mx(w_br_dn), _mx(w_out),
        ln1_g[None, :], ln1_b[None, :], _mx(w_router), b_router, _mx(w_pg), _mx(w_pp))

    dest1, dest2, pend = _dest(route, counts)
    blk = MOE_BLOCK
    n_rows = 2 * T + N_EXPERTS * blk
    padded_end = pend[0, :N_EXPERTS].astype(jnp.int32)
    block_start = jnp.arange(n_rows // blk, dtype=jnp.int32) * blk
    block_expert = jnp.minimum(jnp.sum(block_start[:, None] >= padded_end[None, :], axis=1),
                               N_EXPERTS - 1).astype(jnp.int32)
    n_real = (padded_end[-1:] // blk).astype(jnp.int32)

    xs = _dispatch(x1, dest1, dest2, padded_end, n_rows)
    ys = _experts(xs, block_expert, n_real, layer, w_exp_gate, w_exp_up, w_exp_down)
    return _combine(dest1, dest2, route, base, ys, ln2_g[None, :], ln2_b[None, :])


def kernel(x, p, w_in, b_forget, pool_w, pool_scale, dn_conv, dn_a_log, dn_dt_bias, dn_norm_w, w_br_attn, w_br_pool, w_br_dn, w_out, ln1_g, ln1_b, w_router_group, b_router_group, w_router_expert, b_router_expert, w_exp_gate, w_exp_up, w_exp_down, w_ple_proj, w_ple_gate, ln2_g, ln2_b):
    B, S, _ = x.shape
    x2 = x.reshape(B * S, D_MODEL)
    for i in range(DEPTH):
        x2 = _layer(i, x2, p[i].reshape(B * S, PLE_DIM), B, S, w_in[i], b_forget[i], pool_w[i], pool_scale[i],
                    dn_conv[i], dn_a_log[i], dn_dt_bias[i], dn_norm_w[i], w_br_attn[i], w_br_pool[i],
                    w_br_dn[i], w_out[i], ln1_g[i], ln1_b[i], w_router_group[i], b_router_group[i],
                    w_router_expert[i], b_router_expert[i], w_exp_gate, w_exp_up, w_exp_down,
                    w_ple_proj[i], w_ple_gate[i], ln2_g[i], ln2_b[i])
    return x2.reshape(B, S, D_MODEL)
```

```python
import functools

import numpy as np
import jax
import jax.numpy as jnp
from jax import lax
from jax.experimental import pallas as pl
from jax.experimental.pallas import tpu as pltpu

F32 = jnp.float32
MXU_DTYPE = jnp.bfloat16

D_MODEL = 1024
DEPTH = 4
PLE_DIM = 256
ATTN_HEADS = 8
ATTN_HEAD_DIM = 64
ATTN_WIDTH = ATTN_HEADS * ATTN_HEAD_DIM
POOL_WINDOWS = (2, 4, 8, 16)
POOL_GROUP_DIM = 128
POOL_WIDTH = len(POOL_WINDOWS) * POOL_GROUP_DIM
DN_HEADS = 4
DN_HEAD_DIM = 128
DN_WIDTH = DN_HEADS * DN_HEAD_DIM
DN_CONV = 4
N_GROUPS = 4
EXPERTS_PER_GROUP = 8
N_EXPERTS = N_GROUPS * EXPERTS_PER_GROUP
D_EXPERT = 512
DEEPNORM_ALPHA = (2 * DEPTH) ** 0.25
LN_EPS = 1e-5
NORM_EPS = 1e-6
NEG_INF = -1e30
LOG2_E = 1.4426950408889634

LANES = 128
SUBLANES = 8
VMEM_LIMIT = 56 * 1024 * 1024

FORGET_LANE = 0
DECAY_LANE = ATTN_HEADS
BETA_LANE = ATTN_HEADS + DN_HEADS
GATE_ROWS = 16

TM_PROJ = 1024
TQ_ATTN = 512
TS_POOL = 1024
DN_CHUNK = 128
TM_MERGE = 512
TM_ROWS = 1024
TM_DISPATCH = 2048
TM_DEST = 1024
MOE_BLOCK = 256
CUM_ROWS = 256


def _dot(a, b):
    return jnp.dot(a, b, preferred_element_type=F32)


def _dot_nt(a, b):
    return lax.dot_general(a, b, (((1,), (1,)), ((), ())), preferred_element_type=F32)


def _dot_exact(a, b):
    return jnp.dot(a, b, preferred_element_type=F32, precision=lax.Precision.HIGHEST)


def _mx(a):
    return a.astype(MXU_DTYPE)


def _sigmoid(z):
    return 1.0 / (1.0 + jnp.exp(-z))


def _silu(z):
    return z * _sigmoid(z)


def _layer_norm(h, g, b):
    mu = jnp.mean(h, axis=-1, keepdims=True)
    hc = h - mu
    var = jnp.mean(hc * hc, axis=-1, keepdims=True)
    return hc * lax.rsqrt(var + LN_EPS) * g + b


def _const_spec(shape):
    zeros = (0,) * len(shape)
    return pl.BlockSpec(shape, lambda *_: zeros, pipeline_mode=pl.Buffered(1))


def _params(*sem):
    return pltpu.CompilerParams(dimension_semantics=sem, vmem_limit_bytes=VMEM_LIMIT)


def _inproj_kernel(x_ref, wa_ref, wp_ref, wd_ref, wg_ref, ws_ref,
                   qkv_ref, pu_ref, dqkv_ref, dg_ref, small_ref):
    xb = _mx(x_ref[...])
    qkv_ref[...] = _dot(xb, wa_ref[...]).astype(qkv_ref.dtype)
    pu_ref[...] = _dot(xb, wp_ref[...]).astype(pu_ref.dtype)
    dqkv_ref[...] = _dot(xb, wd_ref[...]).astype(dqkv_ref.dtype)
    dg_ref[...] = _dot(xb, wg_ref[...]).astype(dg_ref.dtype)
    small_ref[...] = _dot(xb, ws_ref[...])


def _inproj(x2, wa, wp, wd, wg, ws):
    T = x2.shape[0]
    tm = TM_PROJ
    row = lambda n: pl.BlockSpec((tm, n), lambda i: (i, 0))
    return pl.pallas_call(
        _inproj_kernel,
        grid=(T // tm,),
        in_specs=[row(D_MODEL), _const_spec(wa.shape), _const_spec(wp.shape), _const_spec(wd.shape),
                  _const_spec(wg.shape), _const_spec(ws.shape)],
        out_specs=[row(3 * ATTN_WIDTH), row(POOL_WIDTH), row(3 * DN_WIDTH), row(DN_WIDTH), row(LANES)],
        out_shape=[jax.ShapeDtypeStruct((T, 3 * ATTN_WIDTH), MXU_DTYPE),
                   jax.ShapeDtypeStruct((T, POOL_WIDTH), MXU_DTYPE),
                   jax.ShapeDtypeStruct((T, 3 * DN_WIDTH), MXU_DTYPE),
                   jax.ShapeDtypeStruct((T, DN_WIDTH), MXU_DTYPE),
                   jax.ShapeDtypeStruct((T, LANES), F32)],
        compiler_params=_params("parallel"),
        name="inproj",
    )(x2, wa, wp, wd, wg, ws)


BIAS_SPLIT_LANES = 3 * ATTN_HEADS
BIAS_ONES_LANE = BIAS_SPLIT_LANES


def _sum_lane(half):
    return (1 - half) * ATTN_HEAD_DIM


def _bias_placement():
    pq = np.zeros((LANES, ATTN_HEADS * LANES), np.float32)
    pk = np.zeros((LANES, ATTN_HEADS * LANES), np.float32)
    for h in range(ATTN_HEADS):
        base = h * LANES + (1 - h % 2) * ATTN_HEAD_DIM
        for part in range(3):
            pq[part * ATTN_HEADS + h, base + part] = 1.0
            pq[BIAS_ONES_LANE, base + 3 + part] = 1.0
            pk[BIAS_ONES_LANE, base + part] = 1.0
            pk[part * ATTN_HEADS + h, base + 3 + part] = -1.0
    return pq, pk


def _gates_kernel(small_ref, bias_ref, alog_ref, qkv_ref, pq_ref, pk_ref,
                  g_ref, gt_ref, qa_ref, ka_ref, va_ref, carry_sc):
    R = small_ref.shape[1]

    @pl.when(pl.program_id(1) == 0)
    def _():
        carry_sc[...] = jnp.zeros(carry_sc.shape, F32)

    ri = lax.broadcasted_iota(jnp.int32, (R, R), 0)
    ci = lax.broadcasted_iota(jnp.int32, (R, R), 1)
    tri_full = (ri >= ci).astype(F32)
    tri_chunk = ((ri >= ci) & (ri // DN_CHUNK == ci // DN_CHUNK)).astype(F32)
    lane = lax.broadcasted_iota(jnp.int32, (R, LANES), 1)
    is_forget = lane < DECAY_LANE
    is_decay = (lane >= DECAY_LANE) & (lane < BETA_LANE)

    z = small_ref[0] + bias_ref[...]
    soft = jnp.log(1.0 + jnp.exp(-jnp.abs(z)))
    log_f = jnp.minimum(z, 0.0) - soft
    decay = -jnp.exp(alog_ref[...]) * (jnp.maximum(z, 0.0) + soft)
    vals = jnp.where(is_forget, log_f, jnp.where(is_decay, decay, _sigmoid(z)))
    cum_full = _dot_exact(tri_full, vals) + carry_sc[...]
    cum_chunk = _dot_exact(tri_chunk, vals)
    carry_sc[...] = cum_full[R - 1:R, :]
    out = jnp.where(is_forget, cum_full, jnp.where(is_decay, cum_chunk, vals))
    g_ref[0] = out
    gt_ref[0] = out.T[:GATE_ROWS, :]

    c = jnp.where(is_forget, out, 0.0) * LOG2_E
    hi = _mx(c).astype(F32)
    mid = _mx(c - hi).astype(F32)
    lo = _mx(c - hi - mid).astype(F32)
    split = (hi + pltpu.roll(mid, ATTN_HEADS, axis=1) + pltpu.roll(lo, 2 * ATTN_HEADS, axis=1)
             + jnp.where(lane == BIAS_ONES_LANE, 1.0, 0.0))
    split_b = _mx(split)
    extra_q = _dot(split_b, pq_ref[...]).astype(qa_ref.dtype)
    extra_k = _dot(split_b, pk_ref[...]).astype(ka_ref.dtype)
    lane1 = lax.broadcasted_iota(jnp.int32, (1, LANES), 1)
    for pair in range(ATTN_HEADS // 2):
        qp = qkv_ref[0, :, pair * LANES:(pair + 1) * LANES]
        kp = qkv_ref[0, :, ATTN_WIDTH + pair * LANES:ATTN_WIDTH + (pair + 1) * LANES]
        vp = qkv_ref[0, :, 2 * ATTN_WIDTH + pair * LANES:2 * ATTN_WIDTH + (pair + 1) * LANES]
        for half in range(2):
            h = 2 * pair + half
            own = (lane1 >= half * ATTN_HEAD_DIM) & (lane1 < (half + 1) * ATTN_HEAD_DIM)
            ones = jnp.where(lane1 == _sum_lane(half), 1.0, 0.0).astype(va_ref.dtype)
            qa_ref[0, h] = jnp.where(own, qp, extra_q[:, h * LANES:(h + 1) * LANES])
            ka_ref[0, h] = jnp.where(own, kp, extra_k[:, h * LANES:(h + 1) * LANES])
            va_ref[0, h] = jnp.where(own, vp, ones)


def _gates(small3, bias_row, alog_row, qkv3):
    B, S, _ = small3.shape
    R = CUM_ROWS
    pq, pk = _bias_placement()
    pq = jnp.asarray(pq, MXU_DTYPE)
    pk = jnp.asarray(pk, MXU_DTYPE)
    H = ATTN_HEADS
    return pl.pallas_call(
        _gates_kernel,
        grid=(B, S // R),
        in_specs=[pl.BlockSpec((1, R, LANES), lambda b, r: (b, r, 0)),
                  pl.BlockSpec((1, LANES), lambda b, r: (0, 0)),
                  pl.BlockSpec((1, LANES), lambda b, r: (0, 0)),
                  pl.BlockSpec((1, R, 3 * ATTN_WIDTH), lambda b, r: (b, r, 0)),
                  pl.BlockSpec(pq.shape, lambda b, r: (0, 0)),
                  pl.BlockSpec(pk.shape, lambda b, r: (0, 0))],
        out_specs=[pl.BlockSpec((1, R, LANES), lambda b, r: (b, r, 0)),
                   pl.BlockSpec((1, GATE_ROWS, R), lambda b, r: (b, 0, r)),
                   pl.BlockSpec((1, H, R, LANES), lambda b, r: (b, 0, r, 0)),
                   pl.BlockSpec((1, H, R, LANES), lambda b, r: (b, 0, r, 0)),
                   pl.BlockSpec((1, H, R, LANES), lambda b, r: (b, 0, r, 0))],
        out_shape=[jax.ShapeDtypeStruct((B, S, LANES), F32),
                   jax.ShapeDtypeStruct((B, GATE_ROWS, S), F32),
                   jax.ShapeDtypeStruct((B, H, S, LANES), MXU_DTYPE),
                   jax.ShapeDtypeStruct((B, H, S, LANES), MXU_DTYPE),
                   jax.ShapeDtypeStruct((B, H, S, LANES), MXU_DTYPE)],
        scratch_shapes=[pltpu.VMEM((1, LANES), F32)],
        compiler_params=_params("parallel", "arbitrary"),
        name="gates",
    )(small3, bias_row, alog_row, qkv3, pq, pk)


def _fox_kernel(qi_ref, kj_ref, q_ref, k_ref, v_ref, o_ref, m_sc, acc_sc):
    step = pl.program_id(1)
    i = qi_ref[step]
    j = kj_ref[step]
    tq = q_ref.shape[2]
    tk = k_ref.shape[2]
    reps = tk // LANES

    @pl.when(j == 0)
    def _():
        m_sc[...] = jnp.full(m_sc.shape, NEG_INF, F32)
        acc_sc[...] = jnp.zeros(acc_sc.shape, F32)

    def sweep(masked):
        if masked:
            rows = lax.broadcasted_iota(jnp.int32, (tq, tk), 0)
            cols = lax.broadcasted_iota(jnp.int32, (tq, tk), 1)
            keep = cols <= rows
        for h in range(ATTN_HEADS):
            s = _dot_nt(q_ref[0, h], k_ref[0, h])
            if masked:
                s = jnp.where(keep, s, NEG_INF)
            m_prev = m_sc[h]
            m_next = jnp.maximum(m_prev, jnp.max(s, axis=1)[:, None])
            p = jnp.exp2(s - jnp.concatenate([m_next] * reps, axis=1))
            m_sc[h] = m_next
            acc_sc[h] = jnp.exp2(m_prev - m_next) * acc_sc[h] + _dot(_mx(p), v_ref[0, h])

    @pl.when(j < i)
    def _():
        sweep(False)

    @pl.when(j == i)
    def _():
        sweep(True)
        lane = lax.broadcasted_iota(jnp.int32, (1, LANES), 1)
        for pair in range(ATTN_HEADS // 2):
            halves = []
            for half in range(2):
                acc = acc_sc[2 * pair + half]
                halves.append(acc / acc[:, _sum_lane(half):_sum_lane(half) + 1])
            out = jnp.where(lane < ATTN_HEAD_DIM, halves[0], halves[1])
            o_ref[0, :, pair * LANES:(pair + 1) * LANES] = out.astype(o_ref.dtype)


def _fox(q_aug, k_aug, v_aug):
    B, H, S, _ = q_aug.shape
    t = TQ_ATTN
    n = S // t
    pairs = [(i, j) for i in range(n) for j in range(i + 1)]
    qi = jnp.asarray([i for i, _ in pairs], jnp.int32)
    kj = jnp.asarray([j for _, j in pairs], jnp.int32)
    grid_spec = pltpu.PrefetchScalarGridSpec(
        num_scalar_prefetch=2,
        grid=(B, len(pairs)),
        in_specs=[pl.BlockSpec((1, H, t, LANES), lambda b, s, qi, kj: (b, 0, qi[s], 0)),
                  pl.BlockSpec((1, H, t, LANES), lambda b, s, qi, kj: (b, 0, kj[s], 0)),
                  pl.BlockSpec((1, H, t, LANES), lambda b, s, qi, kj: (b, 0, kj[s], 0))],
        out_specs=pl.BlockSpec((1, t, ATTN_WIDTH), lambda b, s, qi, kj: (b, qi[s], 0)),
        scratch_shapes=[pltpu.VMEM((ATTN_HEADS, t, LANES), F32),
                        pltpu.VMEM((ATTN_HEADS, t, LANES), F32)],
    )
    return pl.pallas_call(
        _fox_kernel,
        grid_spec=grid_spec,
        out_shape=jax.ShapeDtypeStruct((B, S, ATTN_WIDTH), MXU_DTYPE),
        compiler_params=_params("parallel", "arbitrary"),
        name="fox_attention",
    )(qi, kj, q_aug, k_aug, v_aug)


POOL_HALO = 16


def _pool_kernel(u_ref, w_ref, scale_ref, y_ref, halo_sc):
    s_idx = pl.program_id(1)
    ts = u_ref.shape[1]

    @pl.when(s_idx == 0)
    def _():
        halo_sc[...] = jnp.zeros(halo_sc.shape, F32)

    u = u_ref[0].astype(F32)
    ext = jnp.concatenate([halo_sc[...], u], axis=0)
    halo_sc[...] = u[ts - POOL_HALO:, :]
    pos = (lax.broadcasted_iota(jnp.int32, (ts, 1), 0) + s_idx * ts + 1).astype(F32)
    for g, w in enumerate(POOL_WINDOWS):
        sl = slice(g * POOL_GROUP_DIM, (g + 1) * POOL_GROUP_DIM)
        acc = ext[:, sl]
        span = 1
        while span < w:
            acc = acc + pltpu.roll(acc, span, axis=0)
            span *= 2
        mean = acc[POOL_HALO:, :] / jnp.minimum(pos, float(w))
        d = mean - u[:, sl]
        y = _dot(_mx(d), w_ref[g]) * scale_ref[:, sl]
        y_ref[0, :, sl] = y.astype(y_ref.dtype)


def _pool(pu3, pool_w, pool_scale_row):
    B, S, _ = pu3.shape
    ts = TS_POOL
    return pl.pallas_call(
        _pool_kernel,
        grid=(B, S // ts),
        in_specs=[pl.BlockSpec((1, ts, POOL_WIDTH), lambda b, s: (b, s, 0)),
                  pl.BlockSpec(pool_w.shape, lambda b, s: (0, 0, 0)),
                  pl.BlockSpec((1, POOL_WIDTH), lambda b, s: (0, 0))],
        out_specs=pl.BlockSpec((1, ts, POOL_WIDTH), lambda b, s: (b, s, 0)),
        out_shape=jax.ShapeDtypeStruct((B, S, POOL_WIDTH), MXU_DTYPE),
        scratch_shapes=[pltpu.VMEM((POOL_HALO, POOL_WIDTH), F32)],
        compiler_params=_params("parallel", "arbitrary"),
        name="pool",
    )(pu3, pool_w, pool_scale_row)


DN_HALO = SUBLANES


def _unit_lower_inverses(lows):
    n = lows[0].shape[0]
    eye = (lax.broadcasted_iota(jnp.int32, (n, n), 0) == lax.broadcasted_iota(jnp.int32, (n, n), 1)).astype(F32)
    lows_b = [_mx(low) for low in lows]
    invs = [eye - low for low in lows]
    powers = [_dot(lb, lb) for lb in lows_b]
    span = 2
    while True:
        powers_b = [_mx(pw) for pw in powers]
        invs = [inv + _dot(_mx(inv), pb) for inv, pb in zip(invs, powers_b)]
        span *= 2
        if span >= n:
            return invs
        powers = [_dot(pb, pb) for pb in powers_b]


def _dn_kernel(qkv_ref, dg_ref, g_ref, gt_ref, cw_ref, nw_ref, o_ref, halo_sc, state_sc):
    c_idx = pl.program_id(0)
    NB, C = qkv_ref.shape[0], qkv_ref.shape[1]

    @pl.when(c_idx == 0)
    def _():
        halo_sc[...] = jnp.zeros(halo_sc.shape, F32)
        state_sc[...] = jnp.zeros(state_sc.shape, F32)

    ri = lax.broadcasted_iota(jnp.int32, (C, C), 0)
    ci = lax.broadcasted_iota(jnp.int32, (C, C), 1)
    causal = ri >= ci
    strict = ri > ci

    chains = [(b, h) for b in range(NB) for h in range(DN_HEADS)]
    qs, ks, vs = [], [], []
    for b in range(NB):
        u = qkv_ref[b].astype(F32)
        ext = jnp.concatenate([halo_sc[b], u], axis=0)
        halo_sc[b] = u[C - DN_HALO:, :]
        conv = u * cw_ref[DN_CONV - 1:DN_CONV, :]
        for d in range(1, DN_CONV):
            conv = conv + pltpu.roll(ext, d, axis=0)[DN_HALO:, :] * cw_ref[DN_CONV - 1 - d:DN_CONV - d, :]
        act = _silu(conv)
        for h in range(DN_HEADS):
            q = act[:, h * DN_HEAD_DIM:(h + 1) * DN_HEAD_DIM]
            k = act[:, DN_WIDTH + h * DN_HEAD_DIM:DN_WIDTH + (h + 1) * DN_HEAD_DIM]
            qs.append(q * lax.rsqrt(jnp.sum(q * q, axis=-1, keepdims=True) + NORM_EPS) * (DN_HEAD_DIM ** -0.5))
            ks.append(k * lax.rsqrt(jnp.sum(k * k, axis=-1, keepdims=True) + NORM_EPS))
            vs.append(act[:, 2 * DN_WIDTH + h * DN_HEAD_DIM:2 * DN_WIDTH + (h + 1) * DN_HEAD_DIM])

    gcs = [g_ref[b, :, DECAY_LANE + h:DECAY_LANE + h + 1] for b, h in chains]
    betas = [g_ref[b, :, BETA_LANE + h:BETA_LANE + h + 1] for b, h in chains]
    gc_rows = [gt_ref[b, DECAY_LANE + h:DECAY_LANE + h + 1, :] for b, h in chains]
    g_lasts = [gc[C - 1:C, :] for gc in gcs]
    decays = [jnp.where(causal, jnp.exp(jnp.where(causal, gc - gr, 0.0)), 0.0) for gc, gr in zip(gcs, gc_rows)]
    kbs = [_mx(k) for k in ks]
    kks = [_dot_nt(kb, kb) for kb in kbs]
    qks = [_dot_nt(_mx(q), kb) for q, kb in zip(qs, kbs)]
    lows = [jnp.where(strict, beta * kk * decay, 0.0) for beta, kk, decay in zip(betas, kks, decays)]
    attns = [_mx(jnp.where(causal, qk * decay, 0.0)) for qk, decay in zip(qks, decays)]
    invs = _unit_lower_inverses(lows)
    e_gcs = [jnp.exp(gc) for gc in gcs]
    rhss = [_mx(jnp.concatenate([v * beta, k * (beta * eg)], axis=1)) for v, k, beta, eg in zip(vs, ks, betas, e_gcs)]
    sols = [_dot(_mx(inv), rhs) for inv, rhs in zip(invs, rhss)]
    q_decs = [_mx(q * eg) for q, eg in zip(qs, e_gcs)]
    k_dec_ts = [_mx((k * jnp.exp(gl - gc)).T) for k, gl, gc in zip(ks, g_lasts, gcs)]

    states = [state_sc[b, h] for b, h in chains]
    states_b = [_mx(st) for st in states]
    u_news = [_mx(sol[:, :DN_HEAD_DIM] - _dot(_mx(sol[:, DN_HEAD_DIM:]), sb)) for sol, sb in zip(sols, states_b)]
    outs = [_dot(qd, sb) + _dot(at, un) for qd, sb, at, un in zip(q_decs, states_b, attns, u_news)]
    for (b, h), st, gl, kt, un in zip(chains, states, g_lasts, k_dec_ts, u_news):
        state_sc[b, h] = st * jnp.exp(gl) + _dot(kt, un)
    for (b, h), o in zip(chains, outs):
        sl = slice(h * DN_HEAD_DIM, (h + 1) * DN_HEAD_DIM)
        o = o * lax.rsqrt(jnp.mean(o * o, axis=-1, keepdims=True) + NORM_EPS) * nw_ref[...]
        o = o * _silu(dg_ref[b, :, sl].astype(F32))
        o_ref[b, :, sl] = o.astype(o_ref.dtype)


def _deltanet(dqkv3, dg3, gates, gates_t, conv_w, norm_w_row):
    B, S, _ = dqkv3.shape
    C = DN_CHUNK
    return pl.pallas_call(
        _dn_kernel,
        grid=(S // C,),
        in_specs=[pl.BlockSpec((B, C, 3 * DN_WIDTH), lambda c: (0, c, 0)),
                  pl.BlockSpec((B, C, DN_WIDTH), lambda c: (0, c, 0)),
                  pl.BlockSpec((B, C, LANES), lambda c: (0, c, 0)),
                  pl.BlockSpec((B, GATE_ROWS, C), lambda c: (0, 0, c)),
                  pl.BlockSpec((DN_CONV, 3 * DN_WIDTH), lambda c: (0, 0)),
                  pl.BlockSpec((1, DN_HEAD_DIM), lambda c: (0, 0))],
        out_specs=pl.BlockSpec((B, C, DN_WIDTH), lambda c: (0, c, 0)),
        out_shape=jax.ShapeDtypeStruct((B, S, DN_WIDTH), MXU_DTYPE),
        scratch_shapes=[pltpu.VMEM((B, DN_HALO, 3 * DN_WIDTH), F32),
                        pltpu.VMEM((B, DN_HEADS, DN_HEAD_DIM, DN_HEAD_DIM), F32)],
        compiler_params=_params("arbitrary"),
        name="deltanet",
    )(dqkv3, dg3, gates, gates_t, conv_w, norm_w_row)


ROUTE_E1, ROUTE_E2, ROUTE_G1, ROUTE_G2 = range(4)
ROUTER_EXPERT_LANE = N_GROUPS


def _merge_kernel(x_ref, ya_ref, yp_ref, yd_ref, p_ref, wgate_ref, wa_ref, wp_ref, wd_ref, wo_ref,
                  g1_ref, b1_ref, wr_ref, br_ref, wpg_ref, wpp_ref,
                  x1_ref, base_ref, route_ref, cnt_ref, cnt_sc):
    tm = x_ref.shape[0]

    @pl.when(pl.program_id(0) == 0)
    def _():
        cnt_sc[...] = jnp.zeros(cnt_sc.shape, F32)

    x = x_ref[...]
    gate = _dot(_mx(x), wgate_ref[...])
    merged = (_sigmoid(gate[:, :D_MODEL]) * _dot(ya_ref[...], wa_ref[...])
              + _sigmoid(gate[:, D_MODEL:2 * D_MODEL]) * _dot(yp_ref[...], wp_ref[...])
              + _sigmoid(gate[:, 2 * D_MODEL:]) * _dot(yd_ref[...], wd_ref[...]))
    mix = _dot(_mx(merged), wo_ref[...])
    x1 = _layer_norm(DEEPNORM_ALPHA * x + mix, g1_ref[...], b1_ref[...])
    x1_ref[...] = x1
    x1b = _mx(x1)
    ple = _sigmoid(_dot(x1b, wpg_ref[...])) * _dot(_mx(p_ref[...]), wpp_ref[...])
    base_ref[...] = DEEPNORM_ALPHA * x1 + ple

    logits = _dot(x1b, wr_ref[...]) + br_ref[...]
    lane = lax.broadcasted_iota(jnp.int32, (tm, LANES), 1)
    ninf = -jnp.inf

    def first_argmax(vals):
        top = jnp.max(vals, axis=-1, keepdims=True)
        idx = jnp.min(jnp.where(vals == top, lane, LANES), axis=-1, keepdims=True)
        return top, idx

    group_logits = jnp.where(lane < N_GROUPS, logits, ninf)
    g_top, g_idx = first_argmax(group_logits)
    p_group = 1.0 / jnp.sum(jnp.exp(group_logits - g_top), axis=-1, keepdims=True)
    lo = ROUTER_EXPERT_LANE + EXPERTS_PER_GROUP * g_idx
    local = jnp.where((lane >= lo) & (lane < lo + EXPERTS_PER_GROUP), logits, ninf)
    top1, idx1 = first_argmax(local)
    top2, idx2 = first_argmax(jnp.where(lane == idx1, ninf, local))
    z = jnp.sum(jnp.exp(local - top1), axis=-1, keepdims=True)
    prob1 = 1.0 / z
    prob2 = jnp.exp(top2 - top1) / z
    gate1 = p_group * prob1 / (prob1 + prob2)
    gate2 = p_group * prob2 / (prob1 + prob2)
    e1 = idx1 - ROUTER_EXPERT_LANE
    e2 = idx2 - ROUTER_EXPERT_LANE
    cnt_sc[...] = cnt_sc[...] + jnp.sum(((lane == e1) | (lane == e2)).astype(F32), axis=0, keepdims=True)
    cnt_ref[...] = jnp.broadcast_to(cnt_sc[...], cnt_ref.shape)
    route = jnp.zeros((tm, LANES), F32)
    for pos, val in ((ROUTE_E1, e1.astype(F32)), (ROUTE_E2, e2.astype(F32)),
                     (ROUTE_G1, gate1), (ROUTE_G2, gate2)):
        route = jnp.where(lane == pos, val, route)
    route_ref[...] = route


def _merge(x2, ya, yp, yd, p2, wgate, wa, wp, wd, wo, g1, b1, wr, br, wpg, wpp):
    T = x2.shape[0]
    tm = TM_MERGE
    row = lambda n: pl.BlockSpec((tm, n), lambda i: (i, 0))
    consts = [wgate, wa, wp, wd, wo, g1, b1, wr, br, wpg, wpp]
    return pl.pallas_call(
        _merge_kernel,
        grid=(T // tm,),
        in_specs=[row(D_MODEL), row(ATTN_WIDTH), row(POOL_WIDTH), row(DN_WIDTH), row(PLE_DIM)]
                 + [_const_spec(c.shape) for c in consts],
        out_specs=[row(D_MODEL), row(D_MODEL), row(LANES), pl.BlockSpec((SUBLANES, LANES), lambda i: (0, 0))],
        out_shape=[jax.ShapeDtypeStruct((T, D_MODEL), F32),
                   jax.ShapeDtypeStruct((T, D_MODEL), F32),
                   jax.ShapeDtypeStruct((T, LANES), F32),
                   jax.ShapeDtypeStruct((SUBLANES, LANES), F32)],
        scratch_shapes=[pltpu.VMEM((1, LANES), F32)],
        compiler_params=_params("arbitrary"),
        name="merge_router",
    )(x2, ya, yp, yd, p2, *consts)


def _dest_kernel(route_ref, cnt_ref, d1_ref, d2_ref, pend_ref, start_sc):
    tm = route_ref.shape[0]
    lane = lax.broadcasted_iota(jnp.int32, (tm, LANES), 1)
    e1 = route_ref[:, ROUTE_E1:ROUTE_E1 + 1].astype(jnp.int32)
    e2 = route_ref[:, ROUTE_E2:ROUTE_E2 + 1].astype(jnp.int32)
    onehot = ((lane == e1) | (lane == e2)).astype(F32)

    @pl.when(pl.program_id(0) == 0)
    def _():
        blk = float(MOE_BLOCK)
        padded = jnp.floor((cnt_ref[...] + (blk - 1.0)) / blk) * blk
        ri = lax.broadcasted_iota(jnp.int32, (LANES, LANES), 0)
        ci = lax.broadcasted_iota(jnp.int32, (LANES, LANES), 1)
        ends = _dot_exact(padded, (ri <= ci).astype(F32))
        start_sc[...] = ends[:1] - padded[:1]
        pend_ref[...] = ends

    ri = lax.broadcasted_iota(jnp.int32, (tm, tm), 0)
    ci = lax.broadcasted_iota(jnp.int32, (tm, tm), 1)
    before = _dot((ri > ci).astype(jnp.bfloat16), onehot.astype(jnp.bfloat16)) + start_sc[...]
    dest = jnp.where(lane == 0, jnp.sum(jnp.where(lane == e1, before, 0.0), axis=-1, keepdims=True),
                     jnp.where(lane == 1, jnp.sum(jnp.where(lane == e2, before, 0.0), axis=-1, keepdims=True), 0.0))
    pick = (lax.broadcasted_iota(jnp.int32, (SUBLANES, LANES), 1)
            == lax.broadcasted_iota(jnp.int32, (SUBLANES, LANES), 0)).astype(F32)
    rows = lax.dot_general(pick, dest, (((1,), (1,)), ((), ())), preferred_element_type=F32,
                           precision=lax.Precision.HIGHEST)
    d1_ref[...] = rows[0:1, :].astype(jnp.int32)
    d2_ref[...] = rows[1:2, :].astype(jnp.int32)
    start_sc[...] = start_sc[...] + jnp.sum(onehot, axis=0, keepdims=True)


def _dest(route, counts):
    T = route.shape[0]
    tm = TM_DEST
    return pl.pallas_call(
        _dest_kernel,
        grid=(T // tm,),
        in_specs=[pl.BlockSpec((tm, LANES), lambda i: (i, 0)),
                  pl.BlockSpec((SUBLANES, LANES), lambda i: (0, 0))],
        out_specs=[pl.BlockSpec((1, tm), lambda i: (0, i)),
                   pl.BlockSpec((1, tm), lambda i: (0, i)),
                   pl.BlockSpec((SUBLANES, LANES), lambda i: (0, 0))],
        out_shape=[jax.ShapeDtypeStruct((1, T), jnp.int32),
                   jax.ShapeDtypeStruct((1, T), jnp.int32),
                   jax.ShapeDtypeStruct((SUBLANES, LANES), F32)],
        scratch_shapes=[pltpu.VMEM((1, LANES), F32)],
        compiler_params=_params("arbitrary"),
        name="moe_dest",
    )(route, counts)


ISSUE_UNROLL = 8


def _row_copy(src, src_row, dst, dst_row, sem):
    return pltpu.make_async_copy(src.at[pl.ds(src_row, 1), :], dst.at[pl.ds(dst_row, 1), :], sem)


def _dispatch_kernel(pend_ref, d1_ref, d2_ref, x_ref, xs_ref, zero_sc, sem, zsem):
    tm = x_ref.shape[0]
    n_blocks = xs_ref.shape[0] // MOE_BLOCK
    used_blocks = pend_ref[N_EXPERTS - 1] // MOE_BLOCK

    def fill(block):
        first = pl.multiple_of(block * MOE_BLOCK, MOE_BLOCK)
        return pltpu.make_async_copy(zero_sc, xs_ref.at[pl.ds(first, MOE_BLOCK), :], zsem)

    def fills():
        for e in range(N_EXPERTS):
            yield pend_ref[e] // MOE_BLOCK - 1, pend_ref[e] >= MOE_BLOCK
        for t in range(N_EXPERTS):
            yield used_blocks + t, used_blocks + t < n_blocks

    @pl.when(pl.program_id(0) == 0)
    def _():
        zero_sc[...] = jnp.zeros(zero_sc.shape, F32)
        for block, exists in fills():
            @pl.when(exists)
            def _():
                fill(block).start()
        for block, exists in fills():
            @pl.when(exists)
            def _():
                fill(block).wait()

    def issue(t, carry):
        _row_copy(x_ref, t, xs_ref, d1_ref[0, 0, t], sem.at[0]).start()
        _row_copy(x_ref, t, xs_ref, d2_ref[0, 0, t], sem.at[1]).start()
        return carry

    lax.fori_loop(0, tm, issue, 0, unroll=ISSUE_UNROLL)
    pltpu.make_async_copy(x_ref, xs_ref.at[pl.ds(0, tm), :], sem.at[0]).wait()
    pltpu.make_async_copy(x_ref, xs_ref.at[pl.ds(0, tm), :], sem.at[1]).wait()


def _dispatch(x1, dest1, dest2, padded_end, n_rows):
    T = x1.shape[0]
    tm = TM_DISPATCH
    idx_spec = pl.BlockSpec((1, 1, tm), lambda i, pe: (i, 0, 0), memory_space=pltpu.SMEM)
    grid_spec = pltpu.PrefetchScalarGridSpec(
        num_scalar_prefetch=1,
        grid=(T // tm,),
        in_specs=[idx_spec, idx_spec, pl.BlockSpec((tm, D_MODEL), lambda i, pe: (i, 0))],
        out_specs=pl.BlockSpec(memory_space=pl.ANY),
        scratch_shapes=[pltpu.VMEM((MOE_BLOCK, D_MODEL), F32), pltpu.SemaphoreType.DMA((2,)),
                        pltpu.SemaphoreType.DMA(())],
    )
    return pl.pallas_call(
        _dispatch_kernel,
        grid_spec=grid_spec,
        out_shape=jax.ShapeDtypeStruct((n_rows, D_MODEL), F32),
        compiler_params=_params("arbitrary"),
        name="moe_dispatch",
    )(padded_end, dest1.reshape(T // tm, 1, tm), dest2.reshape(T // tm, 1, tm), x1)


def _experts_kernel(bexp_ref, nreal_ref, xs_ref, wg_ref, wu_ref, wd_ref, ys_ref, wg_sc, wu_sc, wd_sc):
    i = pl.program_id(0)
    changed = (i == 0) | (bexp_ref[i] != bexp_ref[jnp.maximum(i - 1, 0)])

    @pl.when(changed)
    def _():
        wg_sc[...] = _mx(wg_ref[0, 0])
        wu_sc[...] = _mx(wu_ref[0, 0])
        wd_sc[...] = _mx(wd_ref[0, 0])

    @pl.when(i < nreal_ref[0])
    def _():
        xb = _mx(xs_ref[...])
        hid = _silu(_dot(xb, wg_sc[...])) * _dot(xb, wu_sc[...])
        ys_ref[...] = _dot(_mx(hid), wd_sc[...])

    @pl.when(i >= nreal_ref[0])
    def _():
        ys_ref[...] = jnp.zeros(ys_ref.shape, F32)


def _experts(xs, block_expert, n_real, layer, w_gate, w_up, w_down):
    n_rows = xs.shape[0]
    blk = MOE_BLOCK
    grid_spec = pltpu.PrefetchScalarGridSpec(
        num_scalar_prefetch=2,
        grid=(n_rows // blk,),
        in_specs=[pl.BlockSpec((blk, D_MODEL), lambda i, be, nr: (jnp.minimum(i, nr[0] - 1), 0)),
                  pl.BlockSpec((1, 1, D_MODEL, D_EXPERT), lambda i, be, nr: (layer, be[i], 0, 0)),
                  pl.BlockSpec((1, 1, D_MODEL, D_EXPERT), lambda i, be, nr: (layer, be[i], 0, 0)),
                  pl.BlockSpec((1, 1, D_EXPERT, D_MODEL), lambda i, be, nr: (layer, be[i], 0, 0))],
        out_specs=pl.BlockSpec((blk, D_MODEL), lambda i, be, nr: (i, 0)),
        scratch_shapes=[pltpu.VMEM((D_MODEL, D_EXPERT), MXU_DTYPE),
                        pltpu.VMEM((D_MODEL, D_EXPERT), MXU_DTYPE),
                        pltpu.VMEM((D_EXPERT, D_MODEL), MXU_DTYPE)],
    )
    return pl.pallas_call(
        _experts_kernel,
        grid_spec=grid_spec,
        out_shape=jax.ShapeDtypeStruct((n_rows, D_MODEL), F32),
        compiler_params=_params("arbitrary"),
        name="moe_experts",
    )(block_expert, n_real, xs, w_gate, w_up, w_down)


def _combine_kernel(d1_ref, d2_ref, route_ref, base_ref, ys_ref, g2_ref, b2_ref, o_ref, buf, sem):
    tm = base_ref.shape[0]

    def issue(t, carry):
        _row_copy(ys_ref, d1_ref[0, 0, t], buf.at[0], t, sem.at[0]).start()
        _row_copy(ys_ref, d2_ref[0, 0, t], buf.at[1], t, sem.at[1]).start()
        return carry

    lax.fori_loop(0, tm, issue, 0, unroll=ISSUE_UNROLL)
    pltpu.make_async_copy(ys_ref.at[pl.ds(0, tm), :], buf.at[0], sem.at[0]).wait()
    pltpu.make_async_copy(ys_ref.at[pl.ds(0, tm), :], buf.at[1], sem.at[1]).wait()

    gate1 = route_ref[:, ROUTE_G1:ROUTE_G1 + 1]
    gate2 = route_ref[:, ROUTE_G2:ROUTE_G2 + 1]
    moe = gate1 * buf[0] + gate2 * buf[1]
    o_ref[...] = _layer_norm(base_ref[...] + moe, g2_ref[...], b2_ref[...])


def _combine(dest1, dest2, route, base, ys, g2, b2):
    T = base.shape[0]
    tm = TM_ROWS
    idx_spec = pl.BlockSpec((1, 1, tm), lambda i: (i, 0, 0), memory_space=pltpu.SMEM)
    return pl.pallas_call(
        _combine_kernel,
        grid=(T // tm,),
        in_specs=[idx_spec, idx_spec,
                  pl.BlockSpec((tm, LANES), lambda i: (i, 0)),
                  pl.BlockSpec((tm, D_MODEL), lambda i: (i, 0)),
                  pl.BlockSpec(memory_space=pl.ANY),
                  pl.BlockSpec((1, D_MODEL), lambda i: (0, 0)),
                  pl.BlockSpec((1, D_MODEL), lambda i: (0, 0))],
        out_specs=pl.BlockSpec((tm, D_MODEL), lambda i: (i, 0)),
        out_shape=jax.ShapeDtypeStruct((T, D_MODEL), F32),
        scratch_shapes=[pltpu.VMEM((2, tm, D_MODEL), F32), pltpu.SemaphoreType.DMA((2,))],
        compiler_params=_params("arbitrary"),
        name="moe_combine",
    )(dest1.reshape(T // tm, 1, tm), dest2.reshape(T // tm, 1, tm), route, base, ys, g2, b2)


def _pad_lanes(a):
    return jnp.pad(a, ((0, 0), (0, LANES - a.shape[1])))


def _layer(layer, x2, p2, B, S, w_in, b_forget, pool_w, pool_scale, dn_conv, dn_a_log, dn_dt_bias, dn_norm_w,
           w_br_attn, w_br_pool, w_br_dn, w_out, ln1_g, ln1_b, w_rg, b_rg, w_re, b_re,
           w_exp_gate, w_exp_up, w_exp_down, w_pp, w_pg, ln2_g, ln2_b):
    T = B * S
    o = 0
    cols = {}
    for name, width in (("attn", 3 * ATTN_WIDTH), ("forget", ATTN_HEADS), ("pool", POOL_WIDTH),
                        ("dn", 3 * DN_WIDTH), ("decay", DN_HEADS), ("beta", DN_HEADS),
                        ("dgate", DN_WIDTH), ("merge", 3 * D_MODEL)):
        cols[name] = w_in[:, o:o + width]
        o += width
    w_small = _pad_lanes(jnp.concatenate([cols["forget"], cols["decay"], cols["beta"]], axis=1))
    q_scale = jnp.concatenate([jnp.full((ATTN_WIDTH,), LOG2_E * ATTN_HEAD_DIM ** -0.5, F32),
                               jnp.ones((2 * ATTN_WIDTH,), F32)])
    qkv, pu, dqkv, dg, small = _inproj(x2, _mx(cols["attn"] * q_scale), _mx(cols["pool"]), _mx(cols["dn"]),
                                       _mx(cols["dgate"]), _mx(w_small))

    bias_row = _pad_lanes(jnp.concatenate([b_forget, dn_dt_bias])[None, :])
    alog_row = _pad_lanes(jnp.concatenate([jnp.zeros((ATTN_HEADS,), F32), dn_a_log])[None, :])
    gates, gates_t, q_aug, k_aug, v_aug = _gates(small.reshape(B, S, LANES), bias_row, alog_row,
                                              qkv.reshape(B, S, 3 * ATTN_WIDTH))

    y_attn = _fox(q_aug, k_aug, v_aug)
    y_pool = _pool(pu.reshape(B, S, POOL_WIDTH), _mx(pool_w), pool_scale[None, :])
    y_dn = _deltanet(dqkv.reshape(B, S, 3 * DN_WIDTH), dg.reshape(B, S, DN_WIDTH), gates, gates_t,
                     dn_conv, dn_norm_w[None, :])

    w_router = _pad_lanes(jnp.concatenate([w_rg, w_re], axis=1))
    b_router = _pad_lanes(jnp.concatenate([b_rg, b_re])[None, :])
    x1, base, route, counts = _merge(
        x2, y_attn.reshape(T, ATTN_WIDTH), y_pool.reshape(T, POOL_WIDTH), y_dn.reshape(T, DN_WIDTH), p2,
        _mx(cols["merge"]), _mx(w_br_attn), _mx(w_br_pool), _mx(w_br_dn), _mx(w_out),
        ln1_g[None, :], ln1_b[None, :], _mx(w_router), b_router, _mx(w_pg), _mx(w_pp))

    dest1, dest2, pend = _dest(route, counts)
    blk = MOE_BLOCK
    n_rows = 2 * T + N_EXPERTS * blk
    padded_end = pend[0, :N_EXPERTS].astype(jnp.int32)
    block_start = jnp.arange(n_rows // blk, dtype=jnp.int32) * blk
    block_expert = jnp.minimum(jnp.sum(block_start[:, None] >= padded_end[None, :], axis=1),
                               N_EXPERTS - 1).astype(jnp.int32)
    n_real = (padded_end[-1:] // blk).astype(jnp.int32)

    xs = _dispatch(x1, dest1, dest2, padded_end, n_rows)
    ys = _experts(xs, block_expert, n_real, layer, w_exp_gate, w_exp_up, w_exp_down)
    return _combine(dest1, dest2, route, base, ys, ln2_g[None, :], ln2_b[None, :])


def kernel(x, p, w_in, b_forget, pool_w, pool_scale, dn_conv, dn_a_log, dn_dt_bias, dn_norm_w, w_br_attn, w_br_pool, w_br_dn, w_out, ln1_g, ln1_b, w_router_group, b_router_group, w_router_expert, b_router_expert, w_exp_gate, w_exp_up, w_exp_down, w_ple_proj, w_ple_gate, ln2_g, ln2_b):
    B, S, _ = x.shape
    x2 = x.reshape(B * S, D_MODEL)
    for i in range(DEPTH):
        x2 = _layer(i, x2, p[i].reshape(B * S, PLE_DIM), B, S, w_in[i], b_forget[i], pool_w[i], pool_scale[i],
                    dn_conv[i], dn_a_log[i], dn_dt_bias[i], dn_norm_w[i], w_br_attn[i], w_br_pool[i],
                    w_br_dn[i], w_out[i], ln1_g[i], ln1_b[i], w_router_group[i], b_router_group[i],
                    w_router_expert[i], b_router_expert[i], w_exp_gate, w_exp_up, w_exp_down,
                    w_ple_proj[i], w_ple_gate[i], ln2_g[i], ln2_b[i])
    return x2.reshape(B, S, D_MODEL)
```

```python
import functools

import numpy as np
import jax
import jax.numpy as jnp
from jax import lax
from jax.experimental import pallas as pl
from jax.experimental.pallas import tpu as pltpu

F32 = jnp.float32
MXU_DTYPE = jnp.bfloat16

D_MODEL = 1024
DEPTH = 4
PLE_DIM = 256
ATTN_HEADS = 8
ATTN_HEAD_DIM = 64
ATTN_WIDTH = ATTN_HEADS * ATTN_HEAD_DIM
POOL_WINDOWS = (2, 4, 8, 16)
POOL_GROUP_DIM = 128
POOL_WIDTH = len(POOL_WINDOWS) * POOL_GROUP_DIM
DN_HEADS = 4
DN_HEAD_DIM = 128
DN_WIDTH = DN_HEADS * DN_HEAD_DIM
DN_CONV = 4
N_GROUPS = 4
EXPERTS_PER_GROUP = 8
N_EXPERTS = N_GROUPS * EXPERTS_PER_GROUP
D_EXPERT = 512
DEEPNORM_ALPHA = (2 * DEPTH) ** 0.25
LN_EPS = 1e-5
NORM_EPS = 1e-6
NEG_INF = -1e30
LOG2_E = 1.4426950408889634

LANES = 128
SUBLANES = 8
VMEM_LIMIT = 56 * 1024 * 1024

FORGET_LANE = 0
DECAY_LANE = ATTN_HEADS
BETA_LANE = ATTN_HEADS + DN_HEADS
GATE_ROWS = 16

TM_PROJ = 1024
TQ_ATTN = 512
TS_POOL = 1024
DN_CHUNK = 128
TM_MERGE = 512
TM_ROWS = 1024
TM_DISPATCH = 2048
TM_DEST = 1024
MOE_BLOCK = 256
CUM_ROWS = 256


def _dot(a, b):
    return jnp.dot(a, b, preferred_element_type=F32)


def _dot_nt(a, b):
    return lax.dot_general(a, b, (((1,), (1,)), ((), ())), preferred_element_type=F32)


def _dot_exact(a, b):
    return jnp.dot(a, b, preferred_element_type=F32, precision=lax.Precision.HIGHEST)


def _mx(a):
    return a.astype(MXU_DTYPE)


def _sigmoid(z):
    return 1.0 / (1.0 + jnp.exp(-z))


def _silu(z):
    return z * _sigmoid(z)


def _layer_norm(h, g, b):
    mu = jnp.mean(h, axis=-1, keepdims=True)
    hc = h - mu
    var = jnp.mean(hc * hc, axis=-1, keepdims=True)
    return hc * lax.rsqrt(var + LN_EPS) * g + b


def _const_spec(shape):
    zeros = (0,) * len(shape)
    return pl.BlockSpec(shape, lambda *_: zeros, pipeline_mode=pl.Buffered(1))


def _params(*sem):
    return pltpu.CompilerParams(dimension_semantics=sem, vmem_limit_bytes=VMEM_LIMIT)


def _inproj_kernel(x_ref, wa_ref, wp_ref, wd_ref, wg_ref, ws_ref,
                   qkv_ref, pu_ref, dqkv_ref, dg_ref, small_ref):
    xb = _mx(x_ref[...])
    qkv_ref[...] = _dot(xb, wa_ref[...]).astype(qkv_ref.dtype)
    pu_ref[...] = _dot(xb, wp_ref[...]).astype(pu_ref.dtype)
    dqkv_ref[...] = _dot(xb, wd_ref[...]).astype(dqkv_ref.dtype)
    dg_ref[...] = _dot(xb, wg_ref[...]).astype(dg_ref.dtype)
    small_ref[...] = _dot(xb, ws_ref[...])


def _inproj(x2, wa, wp, wd, wg, ws):
    T = x2.shape[0]
    tm = TM_PROJ
    row = lambda n: pl.BlockSpec((tm, n), lambda i: (i, 0))
    return pl.pallas_call(
        _inproj_kernel,
        grid=(T // tm,),
        in_specs=[row(D_MODEL), _const_spec(wa.shape), _const_spec(wp.shape), _const_spec(wd.shape),
                  _const_spec(wg.shape), _const_spec(ws.shape)],
        out_specs=[row(3 * ATTN_WIDTH), row(POOL_WIDTH), row(3 * DN_WIDTH), row(DN_WIDTH), row(LANES)],
        out_shape=[jax.ShapeDtypeStruct((T, 3 * ATTN_WIDTH), MXU_DTYPE),
                   jax.ShapeDtypeStruct((T, POOL_WIDTH), MXU_DTYPE),
                   jax.ShapeDtypeStruct((T, 3 * DN_WIDTH), MXU_DTYPE),
                   jax.ShapeDtypeStruct((T, DN_WIDTH), MXU_DTYPE),
                   jax.ShapeDtypeStruct((T, LANES), F32)],
        compiler_params=_params("parallel"),
        name="inproj",
    )(x2, wa, wp, wd, wg, ws)


BIAS_SPLIT_LANES = 3 * ATTN_HEADS
BIAS_ONES_LANE = BIAS_SPLIT_LANES


def _sum_lane(half):
    return (1 - half) * ATTN_HEAD_DIM


def _bias_placement():
    pq = np.zeros((LANES, ATTN_HEADS * LANES), np.float32)
    pk = np.zeros((LANES, ATTN_HEADS * LANES), np.float32)
    for h in range(ATTN_HEADS):
        base = h * LANES + (1 - h % 2) * ATTN_HEAD_DIM
        for part in range(3):
            pq[part * ATTN_HEADS + h, base + part] = 1.0
            pq[BIAS_ONES_LANE, base + 3 + part] = 1.0
            pk[BIAS_ONES_LANE, base + part] = 1.0
            pk[part * ATTN_HEADS + h, base + 3 + part] = -1.0
    return pq, pk


def _gates_kernel(small_ref, bias_ref, alog_ref, qkv_ref, pq_ref, pk_ref,
                  g_ref, gt_ref, qa_ref, ka_ref, va_ref, carry_sc):
    R = small_ref.shape[1]

    @pl.when(pl.program_id(1) == 0)
    def _():
        carry_sc[...] = jnp.zeros(carry_sc.shape, F32)

    ri = lax.broadcasted_iota(jnp.int32, (R, R), 0)
    ci = lax.broadcasted_iota(jnp.int32, (R, R), 1)
    tri_full = (ri >= ci).astype(F32)
    tri_chunk = ((ri >= ci) & (ri // DN_CHUNK == ci // DN_CHUNK)).astype(F32)
    lane = lax.broadcasted_iota(jnp.int32, (R, LANES), 1)
    is_forget = lane < DECAY_LANE
    is_decay = (lane >= DECAY_LANE) & (lane < BETA_LANE)

    z = small_ref[0] + bias_ref[...]
    soft = jnp.log(1.0 + jnp.exp(-jnp.abs(z)))
    log_f = jnp.minimum(z, 0.0) - soft
    decay = -jnp.exp(alog_ref[...]) * (jnp.maximum(z, 0.0) + soft)
    vals = jnp.where(is_forget, log_f, jnp.where(is_decay, decay, _sigmoid(z)))
    cum_full = _dot_exact(tri_full, vals) + carry_sc[...]
    cum_chunk = _dot_exact(tri_chunk, vals)
    carry_sc[...] = cum_full[R - 1:R, :]
    out = jnp.where(is_forget, cum_full, jnp.where(is_decay, cum_chunk, vals))
    g_ref[0] = out
    gt_ref[0] = out.T[:GATE_ROWS, :]

    c = jnp.where(is_forget, out, 0.0) * LOG2_E
    hi = _mx(c).astype(F32)
    mid = _mx(c - hi).astype(F32)
    lo = _mx(c - hi - mid).astype(F32)
    split = (hi + pltpu.roll(mid, ATTN_HEADS, axis=1) + pltpu.roll(lo, 2 * ATTN_HEADS, axis=1)
             + jnp.where(lane == BIAS_ONES_LANE, 1.0, 0.0))
    split_b = _mx(split)
    extra_q = _dot(split_b, pq_ref[...]).astype(qa_ref.dtype)
    extra_k = _dot(split_b, pk_ref[...]).astype(ka_ref.dtype)
    lane1 = lax.broadcasted_iota(jnp.int32, (1, LANES), 1)
    for pair in range(ATTN_HEADS // 2):
        qp = qkv_ref[0, :, pair * LANES:(pair + 1) * LANES]
        kp = qkv_ref[0, :, ATTN_WIDTH + pair * LANES:ATTN_WIDTH + (pair + 1) * LANES]
        vp = qkv_ref[0, :, 2 * ATTN_WIDTH + pair * LANES:2 * ATTN_WIDTH + (pair + 1) * LANES]
        for half in range(2):
            h = 2 * pair + half
            own = (lane1 >= half * ATTN_HEAD_DIM) & (lane1 < (half + 1) * ATTN_HEAD_DIM)
            ones = jnp.where(lane1 == _sum_lane(half), 1.0, 0.0).astype(va_ref.dtype)
            qa_ref[0, h] = jnp.where(own, qp, extra_q[:, h * LANES:(h + 1) * LANES])
            ka_ref[0, h] = jnp.where(own, kp, extra_k[:, h * LANES:(h + 1) * LANES])
            va_ref[0, h] = jnp.where(own, vp, ones)


def _gates(small3, bias_row, alog_row, qkv3):
    B, S, _ = small3.shape
    R = CUM_ROWS
    pq, pk = _bias_placement()
    pq = jnp.asarray(pq, MXU_DTYPE)
    pk = jnp.asarray(pk, MXU_DTYPE)
    H = ATTN_HEADS
    return pl.pallas_call(
        _gates_kernel,
        grid=(B, S // R),
        in_specs=[pl.BlockSpec((1, R, LANES), lambda b, r: (b, r, 0)),
                  pl.BlockSpec((1, LANES), lambda b, r: (0, 0)),
                  pl.BlockSpec((1, LANES), lambda b, r: (0, 0)),
                  pl.BlockSpec((1, R, 3 * ATTN_WIDTH), lambda b, r: (b, r, 0)),
                  pl.BlockSpec(pq.shape, lambda b, r: (0, 0)),
                  pl.BlockSpec(pk.shape, lambda b, r: (0, 0))],
        out_specs=[pl.BlockSpec((1, R, LANES), lambda b, r: (b, r, 0)),
                   pl.BlockSpec((1, GATE_ROWS, R), lambda b, r: (b, 0, r)),
                   pl.BlockSpec((1, H, R, LANES), lambda b, r: (b, 0, r, 0)),
                   pl.BlockSpec((1, H, R, LANES), lambda b, r: (b, 0, r, 0)),
                   pl.BlockSpec((1, H, R, LANES), lambda b, r: (b, 0, r, 0))],
        out_shape=[jax.ShapeDtypeStruct((B, S, LANES), F32),
                   jax.ShapeDtypeStruct((B, GATE_ROWS, S), F32),
                   jax.ShapeDtypeStruct((B, H, S, LANES), MXU_DTYPE),
                   jax.ShapeDtypeStruct((B, H, S, LANES), MXU_DTYPE),
                   jax.ShapeDtypeStruct((B, H, S, LANES), MXU_DTYPE)],
        scratch_shapes=[pltpu.VMEM((1, LANES), F32)],
        compiler_params=_params("parallel", "arbitrary"),
        name="gates",
    )(small3, bias_row, alog_row, qkv3, pq, pk)


def _fox_kernel(qi_ref, kj_ref, q_ref, k_ref, v_ref, o_ref, m_sc, acc_sc):
    step = pl.program_id(1)
    i = qi_ref[step]
    j = kj_ref[step]
    tq = q_ref.shape[2]
    tk = k_ref.shape[2]
    reps = tk // LANES

    @pl.when(j == 0)
    def _():
        m_sc[...] = jnp.full(m_sc.shape, NEG_INF, F32)
        acc_sc[...] = jnp.zeros(acc_sc.shape, F32)

    def sweep(masked):
        if masked:
            rows = lax.broadcasted_iota(jnp.int32, (tq, tk), 0)
            cols = lax.broadcasted_iota(jnp.int32, (tq, tk), 1)
            keep = cols <= rows
        for h in range(ATTN_HEADS):
            s = _dot_nt(q_ref[0, h], k_ref[0, h])
            if masked:
                s = jnp.where(keep, s, NEG_INF)
            m_prev = m_sc[h]
            m_next = jnp.maximum(m_prev, jnp.max(s, axis=1)[:, None])
            p = jnp.exp2(s - jnp.concatenate([m_next] * reps, axis=1))
            m_sc[h] = m_next
            acc_sc[h] = jnp.exp2(m_prev - m_next) * acc_sc[h] + _dot(_mx(p), v_ref[0, h])

    @pl.when(j < i)
    def _():
        sweep(False)

    @pl.when(j == i)
    def _():
        sweep(True)
        lane = lax.broadcasted_iota(jnp.int32, (1, LANES), 1)
        for pair in range(ATTN_HEADS // 2):
            halves = []
            for half in range(2):
                acc = acc_sc[2 * pair + half]
                halves.append(acc / acc[:, _sum_lane(half):_sum_lane(half) + 1])
            out = jnp.where(lane < ATTN_HEAD_DIM, halves[0], halves[1])
            o_ref[0, :, pair * LANES:(pair + 1) * LANES] = out.astype(o_ref.dtype)


def _fox(q_aug, k_aug, v_aug):
    B, H, S, _ = q_aug.shape
    t = TQ_ATTN
    n = S // t
    pairs = [(i, j) for i in range(n) for j in range(i + 1)]
    qi = jnp.asarray([i for i, _ in pairs], jnp.int32)
    kj = jnp.asarray([j for _, j in pairs], jnp.int32)
    grid_spec = pltpu.PrefetchScalarGridSpec(
        num_scalar_prefetch=2,
        grid=(B, len(pairs)),
        in_specs=[pl.BlockSpec((1, H, t, LANES), lambda b, s, qi, kj: (b, 0, qi[s], 0)),
                  pl.BlockSpec((1, H, t, LANES), lambda b, s, qi, kj: (b, 0, kj[s], 0)),
                  pl.BlockSpec((1, H, t, LANES), lambda b, s, qi, kj: (b, 0, kj[s], 0))],
        out_specs=pl.BlockSpec((1, t, ATTN_WIDTH), lambda b, s, qi, kj: (b, qi[s], 0)),
        scratch_shapes=[pltpu.VMEM((ATTN_HEADS, t, LANES), F32),
                        pltpu.VMEM((ATTN_HEADS, t, LANES), F32)],
    )
    return pl.pallas_call(
        _fox_kernel,
        grid_spec=grid_spec,
        out_shape=jax.ShapeDtypeStruct((B, S, ATTN_WIDTH), MXU_DTYPE),
        compiler_params=_params("parallel", "arbitrary"),
        name="fox_attention",
    )(qi, kj, q_aug, k_aug, v_aug)


POOL_HALO = 16


def _pool_kernel(u_ref, w_ref, scale_ref, y_ref, halo_sc):
    s_idx = pl.program_id(1)
    ts = u_ref.shape[1]

    @pl.when(s_idx == 0)
    def _():
        halo_sc[...] = jnp.zeros(halo_sc.shape, F32)

    u = u_ref[0].astype(F32)
    ext = jnp.concatenate([halo_sc[...], u], axis=0)
    halo_sc[...] = u[ts - POOL_HALO:, :]
    pos = (lax.broadcasted_iota(jnp.int32, (ts, 1), 0) + s_idx * ts + 1).astype(F32)
    for g, w in enumerate(POOL_WINDOWS):
        sl = slice(g * POOL_GROUP_DIM, (g + 1) * POOL_GROUP_DIM)
        acc = ext[:, sl]
        span = 1
        while span < w:
            acc = acc + pltpu.roll(acc, span, axis=0)
            span *= 2
        mean = acc[POOL_HALO:, :] / jnp.minimum(pos, float(w))
        d = mean - u[:, sl]
        y = _dot(_mx(d), w_ref[g]) * scale_ref[:, sl]
        y_ref[0, :, sl] = y.astype(y_ref.dtype)


def _pool(pu3, pool_w, pool_scale_row):
    B, S, _ = pu3.shape
    ts = TS_POOL
    return pl.pallas_call(
        _pool_kernel,
        grid=(B, S // ts),
        in_specs=[pl.BlockSpec((1, ts, POOL_WIDTH), lambda b, s: (b, s, 0)),
                  pl.BlockSpec(pool_w.shape, lambda b, s: (0, 0, 0)),
                  pl.BlockSpec((1, POOL_WIDTH), lambda b, s: (0, 0))],
        out_specs=pl.BlockSpec((1, ts, POOL_WIDTH), lambda b, s: (b, s, 0)),
        out_shape=jax.ShapeDtypeStruct((B, S, POOL_WIDTH), MXU_DTYPE),
        scratch_shapes=[pltpu.VMEM((POOL_HALO, POOL_WIDTH), F32)],
        compiler_params=_params("parallel", "arbitrary"),
        name="pool",
    )(pu3, pool_w, pool_scale_row)


DN_HALO = SUBLANES


def _unit_lower_inverses(lows):
    n = lows[0].shape[0]
    eye = (lax.broadcasted_iota(jnp.int32, (n, n), 0) == lax.broadcasted_iota(jnp.int32, (n, n), 1)).astype(F32)
    lows_b = [_mx(low) for low in lows]
    invs = [eye - low for low in lows]
    powers = [_dot(lb, lb) for lb in lows_b]
    span = 2
    while True:
        powers_b = [_mx(pw) for pw in powers]
        invs = [inv + _dot(_mx(inv), pb) for inv, pb in zip(invs, powers_b)]
        span *= 2
        if span >= n:
            return invs
        powers = [_dot(pb, pb) for pb in powers_b]


def _dn_kernel(qkv_ref, dg_ref, g_ref, gt_ref, cw_ref, nw_ref, o_ref, halo_sc, state_sc):
    c_idx = pl.program_id(0)
    NB, C = qkv_ref.shape[0], qkv_ref.shape[1]

    @pl.when(c_idx == 0)
    def _():
        halo_sc[...] = jnp.zeros(halo_sc.shape, F32)
        state_sc[...] = jnp.zeros(state_sc.shape, F32)

    ri = lax.broadcasted_iota(jnp.int32, (C, C), 0)
    ci = lax.broadcasted_iota(jnp.int32, (C, C), 1)
    causal = ri >= ci
    strict = ri > ci

    chains = [(b, h) for b in range(NB) for h in range(DN_HEADS)]
    qs, ks, vs = [], [], []
    for b in range(NB):
        u = qkv_ref[b].astype(F32)
        ext = jnp.concatenate([halo_sc[b], u], axis=0)
        halo_sc[b] = u[C - DN_HALO:, :]
        conv = u * cw_ref[DN_CONV - 1:DN_CONV, :]
        for d in range(1, DN_CONV):
            conv = conv + pltpu.roll(ext, d, axis=0)[DN_HALO:, :] * cw_ref[DN_CONV - 1 - d:DN_CONV - d, :]
        act = _silu(conv)
        for h in range(DN_HEADS):
            q = act[:, h * DN_HEAD_DIM:(h + 1) * DN_HEAD_DIM]
            k = act[:, DN_WIDTH + h * DN_HEAD_DIM:DN_WIDTH + (h + 1) * DN_HEAD_DIM]
            qs.append(q * lax.rsqrt(jnp.sum(q * q, axis=-1, keepdims=True) + NORM_EPS) * (DN_HEAD_DIM ** -0.5))
            ks.append(k * lax.rsqrt(jnp.sum(k * k, axis=-1, keepdims=True) + NORM_EPS))
            vs.append(act[:, 2 * DN_WIDTH + h * DN_HEAD_DIM:2 * DN_WIDTH + (h + 1) * DN_HEAD_DIM])

    gcs = [g_ref[b, :, DECAY_LANE + h:DECAY_LANE + h + 1] for b, h in chains]
    betas = [g_ref[b, :, BETA_LANE + h:BETA_LANE + h + 1] for b, h in chains]
    gc_rows = [gt_ref[b, DECAY_LANE + h:DECAY_LANE + h + 1, :] for b, h in chains]
    g_lasts = [gc[C - 1:C, :] for gc in gcs]
    decays = [jnp.where(causal, jnp.exp(jnp.where(causal, gc - gr, 0.0)), 0.0) for gc, gr in zip(gcs, gc_rows)]
    kbs = [_mx(k) for k in ks]
    kks = [_dot_nt(kb, kb) for kb in kbs]
    qks = [_dot_nt(_mx(q), kb) for q, kb in zip(qs, kbs)]
    lows = [jnp.where(strict, beta * kk * decay, 0.0) for beta, kk, decay in zip(betas, kks, decays)]
    attns = [_mx(jnp.where(causal, qk * decay, 0.0)) for qk, decay in zip(qks, decays)]
    invs = _unit_lower_inverses(lows)
    e_gcs = [jnp.exp(gc) for gc in gcs]
    rhss = [_mx(jnp.concatenate([v * beta, k * (beta * eg)], axis=1)) for v, k, beta, eg in zip(vs, ks, betas, e_gcs)]
    sols = [_dot(_mx(inv), rhs) for inv, rhs in zip(invs, rhss)]
    q_decs = [_mx(q * eg) for q, eg in zip(qs, e_gcs)]
    k_dec_ts = [_mx((k * jnp.exp(gl - gc)).T) for k, gl, gc in zip(ks, g_lasts, gcs)]

    states = [state_sc[b, h] for b, h in chains]
    states_b = [_mx(st) for st in states]
    u_news = [_mx(sol[:, :DN_HEAD_DIM] - _dot(_mx(sol[:, DN_HEAD_DIM:]), sb)) for sol, sb in zip(sols, states_b)]
    outs = [_dot(qd, sb) + _dot(at, un) for qd, sb, at, un in zip(q_decs, states_b, attns, u_news)]
    for (b, h), st, gl, kt, un in zip(chains, states, g_lasts, k_dec_ts, u_news):
        state_sc[b, h] = st * jnp.exp(gl) + _dot(kt, un)
    for (b, h), o in zip(chains, outs):
        sl = slice(h * DN_HEAD_DIM, (h + 1) * DN_HEAD_DIM)
        o = o * lax.rsqrt(jnp.mean(o * o, axis=-1, keepdims=True) + NORM_EPS) * nw_ref[...]
        o = o * _silu(dg_ref[b, :, sl].astype(F32))
        o_ref[b, :, sl] = o.astype(o_ref.dtype)


def _deltanet(dqkv3, dg3, gates, gates_t, conv_w, norm_w_row):
    B, S, _ = dqkv3.shape
    C = DN_CHUNK
    return pl.pallas_call(
        _dn_kernel,
        grid=(S // C,),
        in_specs=[pl.BlockSpec((B, C, 3 * DN_WIDTH), lambda c: (0, c, 0)),
                  pl.BlockSpec((B, C, DN_WIDTH), lambda c: (0, c, 0)),
                  pl.BlockSpec((B, C, LANES), lambda c: (0, c, 0)),
                  pl.BlockSpec((B, GATE_ROWS, C), lambda c: (0, 0, c)),
                  pl.BlockSpec((DN_CONV, 3 * DN_WIDTH), lambda c: (0, 0)),
                  pl.BlockSpec((1, DN_HEAD_DIM), lambda c: (0, 0))],
        out_specs=pl.BlockSpec((B, C, DN_WIDTH), lambda c: (0, c, 0)),
        out_shape=jax.ShapeDtypeStruct((B, S, DN_WIDTH), MXU_DTYPE),
        scratch_shapes=[pltpu.VMEM((B, DN_HALO, 3 * DN_WIDTH), F32),
                        pltpu.VMEM((B, DN_HEADS, DN_HEAD_DIM, DN_HEAD_DIM), F32)],
        compiler_params=_params("arbitrary"),
        name="deltanet",
    )(dqkv3, dg3, gates, gates_t, conv_w, norm_w_row)


ROUTE_E1, ROUTE_E2, ROUTE_G1, ROUTE_G2 = range(4)
ROUTER_EXPERT_LANE = N_GROUPS


def _merge_kernel(x_ref, ya_ref, yp_ref, yd_ref, p_ref, wgate_ref, wa_ref, wp_ref, wd_ref, wo_ref,
                  g1_ref, b1_ref, wr_ref, br_ref, wpg_ref, wpp_ref,
                  x1_ref, base_ref, route_ref, cnt_ref, cnt_sc):
    tm = x_ref.shape[0]

    @pl.when(pl.program_id(0) == 0)
    def _():
        cnt_sc[...] = jnp.zeros(cnt_sc.shape, F32)

    x = x_ref[...]
    gate = _dot(_mx(x), wgate_ref[...])
    merged = (_sigmoid(gate[:, :D_MODEL]) * _dot(ya_ref[...], wa_ref[...])
              + _sigmoid(gate[:, D_MODEL:2 * D_MODEL]) * _dot(yp_ref[...], wp_ref[...])
              + _sigmoid(gate[:, 2 * D_MODEL:]) * _dot(yd_ref[...], wd_ref[...]))
    mix = _dot(_mx(merged), wo_ref[...])
    x1 = _layer_norm(DEEPNORM_ALPHA * x + mix, g1_ref[...], b1_ref[...])
    x1_ref[...] = x1
    x1b = _mx(x1)
    ple = _sigmoid(_dot(x1b, wpg_ref[...])) * _dot(_mx(p_ref[...]), wpp_ref[...])
    base_ref[...] = DEEPNORM_ALPHA * x1 + ple

    logits = _dot(x1b, wr_ref[...]) + br_ref[...]
    lane = lax.broadcasted_iota(jnp.int32, (tm, LANES), 1)
    ninf = -jnp.inf

    def first_argmax(vals):
        top = jnp.max(vals, axis=-1, keepdims=True)
        idx = jnp.min(jnp.where(vals == top, lane, LANES), axis=-1, keepdims=True)
        return top, idx

    group_logits = jnp.where(lane < N_GROUPS, logits, ninf)
    g_top, g_idx = first_argmax(group_logits)
    p_group = 1.0 / jnp.sum(jnp.exp(group_logits - g_top), axis=-1, keepdims=True)
    lo = ROUTER_EXPERT_LANE + EXPERTS_PER_GROUP * g_idx
    local = jnp.where((lane >= lo) & (lane < lo + EXPERTS_PER_GROUP), logits, ninf)
    top1, idx1 = first_argmax(local)
    top2, idx2 = first_argmax(jnp.where(lane == idx1, ninf, local))
    z = jnp.sum(jnp.exp(local - top1), axis=-1, keepdims=True)
    prob1 = 1.0 / z
    prob2 = jnp.exp(top2 - top1) / z
    gate1 = p_group * prob1 / (prob1 + prob2)
    gate2 = p_group * prob2 / (prob1 + prob2)
    e1 = idx1 - ROUTER_EXPERT_LANE
    e2 = idx2 - ROUTER_EXPERT_LANE
    cnt_sc[...] = cnt_sc[...] + jnp.sum(((lane == e1) | (lane == e2)).astype(F32), axis=0, keepdims=True)
    cnt_ref[...] = jnp.broadcast_to(cnt_sc[...], cnt_ref.shape)
    route = jnp.zeros((tm, LANES), F32)
    for pos, val in ((ROUTE_E1, e1.astype(F32)), (ROUTE_E2, e2.astype(F32)),
                     (ROUTE_G1, gate1), (ROUTE_G2, gate2)):
        route = jnp.where(lane == pos, val, route)
    route_ref[...] = route


def _merge(x2, ya, yp, yd, p2, wgate, wa, wp, wd, wo, g1, b1, wr, br, wpg, wpp):
    T = x2.shape[0]
    tm = TM_MERGE
    row = lambda n: pl.BlockSpec((tm, n), lambda i: (i, 0))
    consts = [wgate, wa, wp, wd, wo, g1, b1, wr, br, wpg, wpp]
    return pl.pallas_call(
        _merge_kernel,
        grid=(T // tm,),
        in_specs=[row(D_MODEL), row(ATTN_WIDTH), row(POOL_WIDTH), row(DN_WIDTH), row(PLE_DIM)]
                 + [_const_spec(c.shape) for c in consts],
        out_specs=[row(D_MODEL), row(D_MODEL), row(LANES), pl.BlockSpec((SUBLANES, LANES), lambda i: (0, 0))],
        out_shape=[jax.ShapeDtypeStruct((T, D_MODEL), F32),
                   jax.ShapeDtypeStruct((T, D_MODEL), F32),
                   jax.ShapeDtypeStruct((T, LANES), F32),
                   jax.ShapeDtypeStruct((SUBLANES, LANES), F32)],
        scratch_shapes=[pltpu.VMEM((1, LANES), F32)],
        compiler_params=_params("arbitrary"),
        name="merge_router",
    )(x2, ya, yp, yd, p2, *consts)


def _dest_kernel(route_ref, cnt_ref, d1_ref, d2_ref, pend_ref, start_sc):
    tm = route_ref.shape[0]
    lane = lax.broadcasted_iota(jnp.int32, (tm, LANES), 1)
    e1 = route_ref[:, ROUTE_E1:ROUTE_E1 + 1].astype(jnp.int32)
    e2 = route_ref[:, ROUTE_E2:ROUTE_E2 + 1].astype(jnp.int32)
    onehot = ((lane == e1) | (lane == e2)).astype(F32)

    @pl.when(pl.program_id(0) == 0)
    def _():
        blk = float(MOE_BLOCK)
        padded = jnp.floor((cnt_ref[...] + (blk - 1.0)) / blk) * blk
        ri = lax.broadcasted_iota(jnp.int32, (LANES, LANES), 0)
        ci = lax.broadcasted_iota(jnp.int32, (LANES, LANES), 1)
        ends = _dot_exact(padded, (ri <= ci).astype(F32))
        start_sc[...] = ends[:1] - padded[:1]
        pend_ref[...] = ends

    ri = lax.broadcasted_iota(jnp.int32, (tm, tm), 0)
    ci = lax.broadcasted_iota(jnp.int32, (tm, tm), 1)
    before = _dot((ri > ci).astype(jnp.bfloat16), onehot.astype(jnp.bfloat16)) + start_sc[...]
    dest = jnp.where(lane == 0, jnp.sum(jnp.where(lane == e1, before, 0.0), axis=-1, keepdims=True),
                     jnp.where(lane == 1, jnp.sum(jnp.where(lane == e2, before, 0.0), axis=-1, keepdims=True), 0.0))
    pick = (lax.broadcasted_iota(jnp.int32, (SUBLANES, LANES), 1)
            == lax.broadcasted_iota(jnp.int32, (SUBLANES, LANES), 0)).astype(F32)
    rows = lax.dot_general(pick, dest, (((1,), (1,)), ((), ())), preferred_element_type=F32,
                           precision=lax.Precision.HIGHEST)
    d1_ref[...] = rows[0:1, :].astype(jnp.int32)
    d2_ref[...] = rows[1:2, :].astype(jnp.int32)
    start_sc[...] = start_sc[...] + jnp.sum(onehot, axis=0, keepdims=True)


def _dest(route, counts):
    T = route.shape[0]
    tm = TM_DEST
    return pl.pallas_call(
        _dest_kernel,
        grid=(T // tm,),
        in_specs=[pl.BlockSpec((tm, LANES), lambda i: (i, 0)),
                  pl.BlockSpec((SUBLANES, LANES), lambda i: (0, 0))],
        out_specs=[pl.BlockSpec((1, tm), lambda i: (0, i)),
                   pl.BlockSpec((1, tm), lambda i: (0, i)),
                   pl.BlockSpec((SUBLANES, LANES), lambda i: (0, 0))],
        out_shape=[jax.ShapeDtypeStruct((1, T), jnp.int32),
                   jax.ShapeDtypeStruct((1, T), jnp.int32),
                   jax.ShapeDtypeStruct((SUBLANES, LANES), F32)],
        scratch_shapes=[pltpu.VMEM((1, LANES), F32)],
        compiler_params=_params("arbitrary"),
        name="moe_dest",
    )(route, counts)


ISSUE_UNROLL = 8
COMBINE_PARTS = 4


def _row_copy(src, src_row, dst, dst_row, sem):
    return pltpu.make_async_copy(src.at[pl.ds(src_row, 1), :], dst.at[pl.ds(dst_row, 1), :], sem)


def _dispatch_kernel(pend_ref, d1_ref, d2_ref, x_ref, xs_ref, zero_sc, sem, zsem):
    tm = x_ref.shape[0]
    n_blocks = xs_ref.shape[0] // MOE_BLOCK
    used_blocks = pend_ref[N_EXPERTS - 1] // MOE_BLOCK

    def fill(block):
        first = pl.multiple_of(block * MOE_BLOCK, MOE_BLOCK)
        return pltpu.make_async_copy(zero_sc, xs_ref.at[pl.ds(first, MOE_BLOCK), :], zsem)

    def fills():
        for e in range(N_EXPERTS):
            yield pend_ref[e] // MOE_BLOCK - 1, pend_ref[e] >= MOE_BLOCK
        for t in range(N_EXPERTS):
            yield used_blocks + t, used_blocks + t < n_blocks

    @pl.when(pl.program_id(0) == 0)
    def _():
        zero_sc[...] = jnp.zeros(zero_sc.shape, F32)
        for block, exists in fills():
            @pl.when(exists)
            def _():
                fill(block).start()
        for block, exists in fills():
            @pl.when(exists)
            def _():
                fill(block).wait()

    def issue(t, carry):
        _row_copy(x_ref, t, xs_ref, d1_ref[0, 0, t], sem.at[0]).start()
        _row_copy(x_ref, t, xs_ref, d2_ref[0, 0, t], sem.at[1]).start()
        return carry

    lax.fori_loop(0, tm, issue, 0, unroll=ISSUE_UNROLL)
    pltpu.make_async_copy(x_ref, xs_ref.at[pl.ds(0, tm), :], sem.at[0]).wait()
    pltpu.make_async_copy(x_ref, xs_ref.at[pl.ds(0, tm), :], sem.at[1]).wait()


def _dispatch(x1, dest1, dest2, padded_end, n_rows):
    T = x1.shape[0]
    tm = TM_DISPATCH
    idx_spec = pl.BlockSpec((1, 1, tm), lambda i, pe: (i, 0, 0), memory_space=pltpu.SMEM)
    grid_spec = pltpu.PrefetchScalarGridSpec(
        num_scalar_prefetch=1,
        grid=(T // tm,),
        in_specs=[idx_spec, idx_spec, pl.BlockSpec((tm, D_MODEL), lambda i, pe: (i, 0))],
        out_specs=pl.BlockSpec(memory_space=pl.ANY),
        scratch_shapes=[pltpu.VMEM((MOE_BLOCK, D_MODEL), F32), pltpu.SemaphoreType.DMA((2,)),
                        pltpu.SemaphoreType.DMA(())],
    )
    return pl.pallas_call(
        _dispatch_kernel,
        grid_spec=grid_spec,
        out_shape=jax.ShapeDtypeStruct((n_rows, D_MODEL), F32),
        compiler_params=_params("arbitrary"),
        name="moe_dispatch",
    )(padded_end, dest1.reshape(T // tm, 1, tm), dest2.reshape(T // tm, 1, tm), x1)


def _experts_kernel(bexp_ref, nreal_ref, xs_ref, wg_ref, wu_ref, wd_ref, ys_ref, wg_sc, wu_sc, wd_sc):
    i = pl.program_id(0)
    changed = (i == 0) | (bexp_ref[i] != bexp_ref[jnp.maximum(i - 1, 0)])

    @pl.when(changed)
    def _():
        wg_sc[...] = _mx(wg_ref[0, 0])
        wu_sc[...] = _mx(wu_ref[0, 0])
        wd_sc[...] = _mx(wd_ref[0, 0])

    @pl.when(i < nreal_ref[0])
    def _():
        xb = _mx(xs_ref[...])
        hid = _silu(_dot(xb, wg_sc[...])) * _dot(xb, wu_sc[...])
        ys_ref[...] = _dot(_mx(hid), wd_sc[...])

    @pl.when(i >= nreal_ref[0])
    def _():
        ys_ref[...] = jnp.zeros(ys_ref.shape, F32)


def _experts(xs, block_expert, n_real, layer, w_gate, w_up, w_down):
    n_rows = xs.shape[0]
    blk = MOE_BLOCK
    grid_spec = pltpu.PrefetchScalarGridSpec(
        num_scalar_prefetch=2,
        grid=(n_rows // blk,),
        in_specs=[pl.BlockSpec((blk, D_MODEL), lambda i, be, nr: (jnp.minimum(i, nr[0] - 1), 0)),
                  pl.BlockSpec((1, 1, D_MODEL, D_EXPERT), lambda i, be, nr: (layer, be[i], 0, 0)),
                  pl.BlockSpec((1, 1, D_MODEL, D_EXPERT), lambda i, be, nr: (layer, be[i], 0, 0)),
                  pl.BlockSpec((1, 1, D_EXPERT, D_MODEL), lambda i, be, nr: (layer, be[i], 0, 0))],
        out_specs=pl.BlockSpec((blk, D_MODEL), lambda i, be, nr: (i, 0)),
        scratch_shapes=[pltpu.VMEM((D_MODEL, D_EXPERT), MXU_DTYPE),
                        pltpu.VMEM((D_MODEL, D_EXPERT), MXU_DTYPE),
                        pltpu.VMEM((D_EXPERT, D_MODEL), MXU_DTYPE)],
    )
    return pl.pallas_call(
        _experts_kernel,
        grid_spec=grid_spec,
        out_shape=jax.ShapeDtypeStruct((n_rows, D_MODEL), F32),
        compiler_params=_params("arbitrary"),
        name="moe_experts",
    )(block_expert, n_real, xs, w_gate, w_up, w_down)


def _combine_kernel(d1_ref, d2_ref, route_ref, base_ref, ys_ref, g2_ref, b2_ref, o_ref, buf, sem):
    tm = base_ref.shape[0]
    rows_per_part = tm // COMBINE_PARTS

    def issue_part(part):
        def issue(t, carry):
            _row_copy(ys_ref, d1_ref[0, 0, t], buf.at[0], t, sem.at[0, part]).start()
            _row_copy(ys_ref, d2_ref[0, 0, t], buf.at[1], t, sem.at[1, part]).start()
            return carry

        lax.fori_loop(part * rows_per_part, (part + 1) * rows_per_part, issue, 0, unroll=ISSUE_UNROLL)

    def finish_part(part):
        rows = pl.ds(part * rows_per_part, rows_per_part)
        for k in range(2):
            pltpu.make_async_copy(ys_ref.at[pl.ds(0, rows_per_part), :], buf.at[k, rows], sem.at[k, part]).wait()
        gate1 = route_ref[rows, ROUTE_G1:ROUTE_G1 + 1]
        gate2 = route_ref[rows, ROUTE_G2:ROUTE_G2 + 1]
        moe = gate1 * buf[0, rows, :] + gate2 * buf[1, rows, :]
        o_ref[rows, :] = _layer_norm(base_ref[rows, :] + moe, g2_ref[...], b2_ref[...])

    for part in range(COMBINE_PARTS):
        issue_part(part)
    for part in range(COMBINE_PARTS):
        finish_part(part)


def _combine(dest1, dest2, route, base, ys, g2, b2):
    T = base.shape[0]
    tm = TM_ROWS
    idx_spec = pl.BlockSpec((1, 1, tm), lambda i: (i, 0, 0), memory_space=pltpu.SMEM)
    return pl.pallas_call(
        _combine_kernel,
        grid=(T // tm,),
        in_specs=[idx_spec, idx_spec,
                  pl.BlockSpec((tm, LANES), lambda i: (i, 0)),
                  pl.BlockSpec((tm, D_MODEL), lambda i: (i, 0)),
                  pl.BlockSpec(memory_space=pl.ANY),
                  pl.BlockSpec((1, D_MODEL), lambda i: (0, 0)),
                  pl.BlockSpec((1, D_MODEL), lambda i: (0, 0))],
        out_specs=pl.BlockSpec((tm, D_MODEL), lambda i: (i, 0)),
        out_shape=jax.ShapeDtypeStruct((T, D_MODEL), F32),
        scratch_shapes=[pltpu.VMEM((2, tm, D_MODEL), F32), pltpu.SemaphoreType.DMA((2, COMBINE_PARTS))],
        compiler_params=_params("arbitrary"),
        name="moe_combine",
    )(dest1.reshape(T // tm, 1, tm), dest2.reshape(T // tm, 1, tm), route, base, ys, g2, b2)


def _pad_lanes(a):
    return jnp.pad(a, ((0, 0), (0, LANES - a.shape[1])))


def _layer(layer, x2, p2, B, S, w_in, b_forget, pool_w, pool_scale, dn_conv, dn_a_log, dn_dt_bias, dn_norm_w,
           w_br_attn, w_br_pool, w_br_dn, w_out, ln1_g, ln1_b, w_rg, b_rg, w_re, b_re,
           w_exp_gate, w_exp_up, w_exp_down, w_pp, w_pg, ln2_g, ln2_b):
    T = B * S
    o = 0
    cols = {}
    for name, width in (("attn", 3 * ATTN_WIDTH), ("forget", ATTN_HEADS), ("pool", POOL_WIDTH),
                        ("dn", 3 * DN_WIDTH), ("decay", DN_HEADS), ("beta", DN_HEADS),
                        ("dgate", DN_WIDTH), ("merge", 3 * D_MODEL)):
        cols[name] = w_in[:, o:o + width]
        o += width
    w_small = _pad_lanes(jnp.concatenate([cols["forget"], cols["decay"], cols["beta"]], axis=1))
    q_scale = jnp.concatenate([jnp.full((ATTN_WIDTH,), LOG2_E * ATTN_HEAD_DIM ** -0.5, F32),
                               jnp.ones((2 * ATTN_WIDTH,), F32)])
    qkv, pu, dqkv, dg, small = _inproj(x2, _mx(cols["attn"] * q_scale), _mx(cols["pool"]), _mx(cols["dn"]),
                                       _mx(cols["dgate"]), _mx(w_small))

    bias_row = _pad_lanes(jnp.concatenate([b_forget, dn_dt_bias])[None, :])
    alog_row = _pad_lanes(jnp.concatenate([jnp.zeros((ATTN_HEADS,), F32), dn_a_log])[None, :])
    gates, gates_t, q_aug, k_aug, v_aug = _gates(small.reshape(B, S, LANES), bias_row, alog_row,
                                              qkv.reshape(B, S, 3 * ATTN_WIDTH))

    y_attn = _fox(q_aug, k_aug, v_aug)
    y_pool = _pool(pu.reshape(B, S, POOL_WIDTH), _mx(pool_w), pool_scale[None, :])
    y_dn = _deltanet(dqkv.reshape(B, S, 3 * DN_WIDTH), dg.reshape(B, S, DN_WIDTH), gates, gates_t,
                     dn_conv, dn_norm_w[None, :])

    w_router = _pad_lanes(jnp.concatenate([w_rg, w_re], axis=1))
    b_router = _pad_lanes(jnp.concatenate([b_rg, b_re])[None, :])
    x1, base, route, counts = _merge(
        x2, y_attn.reshape(T, ATTN_WIDTH), y_pool.reshape(T, POOL_WIDTH), y_dn.reshape(T, DN_WIDTH), p2,
        _mx(cols["merge"]), _mx(w_br_attn), _mx(w_br_pool), _mx(w_br_dn), _mx(w_out),
        ln1_g[None, :], ln1_b[None, :], _mx(w_router), b_router, _mx(w_pg), _mx(w_pp))

    dest1, dest2, pend = _dest(route, counts)
    blk = MOE_BLOCK
    n_rows = 2 * T + N_EXPERTS * blk
    padded_end = pend[0, :N_EXPERTS].astype(jnp.int32)
    block_start = jnp.arange(n_rows // blk, dtype=jnp.int32) * blk
    block_expert = jnp.minimum(jnp.sum(block_start[:, None] >= padded_end[None, :], axis=1),
                               N_EXPERTS - 1).astype(jnp.int32)
    n_real = (padded_end[-1:] // blk).astype(jnp.int32)

    xs = _dispatch(x1, dest1, dest2, padded_end, n_rows)
    ys = _experts(xs, block_expert, n_real, layer, w_exp_gate, w_exp_up, w_exp_down)
    return _combine(dest1, dest2, route, base, ys, ln2_g[None, :], ln2_b[None, :])


def kernel(x, p, w_in, b_forget, pool_w, pool_scale, dn_conv, dn_a_log, dn_dt_bias, dn_norm_w, w_br_attn, w_br_pool, w_br_dn, w_out, ln1_g, ln1_b, w_router_group, b_router_group, w_router_expert, b_router_expert, w_exp_gate, w_exp_up, w_exp_down, w_ple_proj, w_ple_gate, ln2_g, ln2_b):
    B, S, _ = x.shape
    x2 = x.reshape(B * S, D_MODEL)
    for i in range(DEPTH):
        x2 = _layer(i, x2, p[i].reshape(B * S, PLE_DIM), B, S, w_in[i], b_forget[i], pool_w[i], pool_scale[i],
                    dn_conv[i], dn_a_log[i], dn_dt_bias[i], dn_norm_w[i], w_br_attn[i], w_br_pool[i],
                    w_br_dn[i], w_out[i], ln1_g[i], ln1_b[i], w_router_group[i], b_router_group[i],
                    w_router_expert[i], b_router_expert[i], w_exp_gate, w_exp_up, w_exp_down,
                    w_ple_proj[i], w_ple_gate[i], ln2_g[i], ln2_b[i])
    return x2.reshape(B, S, D_MODEL)
```
